```python
import jax, jax.numpy as jnp
from jax import lax
import numpy as np

D_MODEL = 1024
BATCH = 4
SEQ = 4096
DEPTH = 4
DEC_BATCH = 2
DEC_SEQ = 8192
PAST_LEN = 128

D_FF = 2816
NORM_EPS = 1e-6
ROPE_THETA = 10000.0
BLOCK = 128

MLA_HEADS = 8
MLA_Q_RANK = 256
MLA_KV_RANK = 128
MLA_NOPE = 64
MLA_ROPE = 32
MLA_V = 64

CONV_WIDTH = 512
CONV_K = 3

SWA_HEADS = 8
SWA_KV_HEADS = 2
SWA_HEAD_DIM = 64
SWA_WINDOW = 128

N_BRANCHES = 3

MLA_COLS = MLA_Q_RANK + MLA_KV_RANK + MLA_ROPE
CONV_COLS = 3 * CONV_WIDTH
SWA_Q_COLS = SWA_HEADS * SWA_HEAD_DIM
SWA_KV_COLS = SWA_KV_HEADS * SWA_HEAD_DIM
SWA_COLS = SWA_Q_COLS + 2 * SWA_KV_COLS
GATE_COLS = N_BRANCHES * D_MODEL
IN_COLS = MLA_COLS + CONV_COLS + SWA_COLS + GATE_COLS

kernel_name = "hybrid_mla_conv_swa_encoder"


def rmsnorm(x, g):
    xf = x.astype(jnp.float32)
    y = xf * lax.rsqrt(jnp.mean(xf * xf, axis=-1, keepdims=True) + NORM_EPS)
    return (y * g.astype(jnp.float32)).astype(x.dtype)


def rope_tables(seq, dim):
    inv = 1.0 / (ROPE_THETA ** (jnp.arange(0, dim, 2, dtype=jnp.float32) / dim))
    ang = jnp.arange(seq, dtype=jnp.float32)[:, None] * inv[None, :]
    return jnp.cos(ang), jnp.sin(ang)


def apply_rope(x, cos, sin):
    xf = x.astype(jnp.float32)
    x1, x2 = jnp.split(xf, 2, axis=-1)
    out = jnp.concatenate([x1 * cos - x2 * sin, x2 * cos + x1 * sin], axis=-1)
    return out.astype(x.dtype)


def swiglu_ffn(x, g, w_gu, w_down):
    h = rmsnorm(x, g) @ w_gu
    a, b = jnp.split(h, 2, axis=-1)
    return (jax.nn.silu(a) * b) @ w_down


def mla_mixer(q_lat, kv_lat, k_rope_raw, q_norm, w_uq, kv_norm, w_ukv, w_out, cos, sin):
    B, S, _ = q_lat.shape
    nb = S // BLOCK
    q = (rmsnorm(q_lat, q_norm) @ w_uq).reshape(B, S, MLA_HEADS, MLA_NOPE + MLA_ROPE)
    q_nope = q[..., :MLA_NOPE]
    q_rope = apply_rope(q[..., MLA_NOPE:], cos[:, None, :], sin[:, None, :])
    kv = (rmsnorm(kv_lat, kv_norm) @ w_ukv).reshape(B, S, MLA_HEADS, MLA_NOPE + MLA_V)
    k_nope = kv[..., :MLA_NOPE]
    v = kv[..., MLA_NOPE:]
    k_rope = apply_rope(k_rope_raw, cos, sin)
    scale = (MLA_NOPE + MLA_ROPE) ** -0.5
    qn = q_nope.reshape(B, nb, BLOCK, MLA_HEADS, MLA_NOPE).transpose(1, 0, 2, 3, 4)
    qr = q_rope.reshape(B, nb, BLOCK, MLA_HEADS, MLA_ROPE).transpose(1, 0, 2, 3, 4)

    def attend(qb):
        qn_b, qr_b = qb
        s = (jnp.einsum('bqhd,bkhd->bhqk', qn_b, k_nope)
             + jnp.einsum('bqhr,bkr->bhqk', qr_b, k_rope)).astype(jnp.float32) * scale
        p = jax.nn.softmax(s, axis=-1).astype(v.dtype)
        return jnp.einsum('bhqk,bkhd->bqhd', p, v)

    o = lax.map(attend, (qn, qr))
    o = o.transpose(1, 0, 2, 3, 4).reshape(B, S, MLA_HEADS * MLA_V)
    return o @ w_out


def short_conv_mixer(cols, conv_w, w_out):
    S = cols.shape[1]
    b_gate, c_gate, x_in = jnp.split(cols, 3, axis=-1)
    z = c_gate * x_in
    half = CONV_K // 2
    zp = jnp.pad(z, ((0, 0), (half, half), (0, 0)))
    y = sum(zp[:, k:k + S, :] * conv_w[k] for k in range(CONV_K))
    return (b_gate * y) @ w_out


def swa_mixer(q, k, v, sink, w_out, cos, sin):
    B, S = q.shape[0], q.shape[1]
    nb = S // BLOCK
    G = SWA_HEADS // SWA_KV_HEADS
    q = apply_rope(q.reshape(B, S, SWA_HEADS, SWA_HEAD_DIM), cos[:, None, :], sin[:, None, :])
    k = apply_rope(k.reshape(B, S, SWA_KV_HEADS, SWA_HEAD_DIM), cos[:, None, :], sin[:, None, :])
    v = v.reshape(B, S, SWA_KV_HEADS, SWA_HEAD_DIM)
    qb = q.reshape(B, nb, BLOCK, SWA_KV_HEADS, G, SWA_HEAD_DIM)

    def band(t):
        tp = jnp.pad(t, ((0, 0), (BLOCK, BLOCK), (0, 0), (0, 0)))
        tp = tp.reshape(B, nb + 2, BLOCK, SWA_KV_HEADS, SWA_HEAD_DIM)
        return jnp.concatenate([tp[:, :-2], tp[:, 1:-1], tp[:, 2:]], axis=2)

    kw, vw = band(k), band(v)
    a = jnp.arange(BLOCK)[:, None]
    c = jnp.arange(3 * BLOCK)[None, :]
    rel = c - BLOCK - a
    j = jnp.arange(nb)[:, None, None] * BLOCK - BLOCK + c[None]
    mask = (jnp.abs(rel) <= SWA_WINDOW)[None] & (j >= 0) & (j < S)
    scale = SWA_HEAD_DIM ** -0.5
    s = jnp.einsum('bnqhgd,bnkhd->bnhgqk', qb, kw).astype(jnp.float32) * scale
    s = jnp.where(mask[None, :, None, None], s, -1e30)
    sk = sink.astype(jnp.float32).reshape(SWA_KV_HEADS, G)[None, None, :, :, None, None]
    m = jnp.maximum(jnp.max(s, axis=-1, keepdims=True), sk)
    p = jnp.exp(s - m)
    denom = jnp.sum(p, axis=-1, keepdims=True) + jnp.exp(sk - m)
    p = (p / denom).astype(v.dtype)
    o = jnp.einsum('bnhgqk,bnkhd->bnqhgd', p, vw).reshape(B, S, SWA_HEADS * SWA_HEAD_DIM)
    return o @ w_out


def encoder_layer(x, p, l, mla_cs, swa_cs):
    B, S, D = x.shape
    x = x + 0.5 * swiglu_ffn(x, p['ffn1_norm'][l], p['ffn1_w_gu'][l], p['ffn1_w_down'][l])
    u = rmsnorm(x, p['mix_norm'][l])
    cols = u @ p['w_in'][l]
    o0 = 0
    q_lat = cols[..., o0:o0 + MLA_Q_RANK]; o0 += MLA_Q_RANK
    kv_lat = cols[..., o0:o0 + MLA_KV_RANK]; o0 += MLA_KV_RANK
    k_rope_raw = cols[..., o0:o0 + MLA_ROPE]; o0 += MLA_ROPE
    conv_cols = cols[..., o0:o0 + CONV_COLS]; o0 += CONV_COLS
    sq = cols[..., o0:o0 + SWA_Q_COLS]; o0 += SWA_Q_COLS
    sk = cols[..., o0:o0 + SWA_KV_COLS]; o0 += SWA_KV_COLS
    sv = cols[..., o0:o0 + SWA_KV_COLS]; o0 += SWA_KV_COLS
    gate_cols = cols[..., o0:o0 + GATE_COLS]

    y_a = mla_mixer(q_lat, kv_lat, k_rope_raw, p['mla_q_norm'][l], p['mla_w_uq'][l],
                    p['mla_kv_norm'][l], p['mla_w_ukv'][l], p['mla_w_o'][l], *mla_cs)
    y_b = short_conv_mixer(conv_cols, p['conv_w'][l], p['conv_w_o'][l])
    y_c = swa_mixer(sq, sk, sv, p['swa_sink'][l], p['swa_w_o'][l], *swa_cs)

    gates = jax.nn.sigmoid(gate_cols.astype(jnp.float32)).astype(x.dtype).reshape(B, S, N_BRANCHES, D)
    merged = gates[:, :, 0] * y_a + gates[:, :, 1] * y_b + gates[:, :, 2] * y_c
    x = x + merged @ p['w_o'][l]
    x = x + 0.5 * swiglu_ffn(x, p['ffn2_norm'][l], p['ffn2_w_gu'][l], p['ffn2_w_down'][l])
    return x


def encode(x, p, final_norm):
    S = x.shape[1]
    mla_cs = rope_tables(S, MLA_ROPE)
    swa_cs = rope_tables(S, SWA_HEAD_DIM)
    for l in range(DEPTH):
        x = encoder_layer(x, p, l, mla_cs, swa_cs)
    return rmsnorm(x, final_norm)


def setup_inputs(seed: int = 0) -> dict:
    key = jax.random.key(seed)
    ks = jax.random.split(key, 24)

    def w(k, shape, fan_in):
        return jax.random.normal(k, shape, jnp.float32) * (fan_in ** -0.5)

    def gain(k, shape):
        return 1.0 + 0.02 * jax.random.normal(k, shape, jnp.float32)

    L = DEPTH
    return {
        "x_prompt": jax.random.normal(ks[0], (BATCH, SEQ, D_MODEL), jnp.float32),
        "x_sample": jax.random.normal(ks[1], (DEC_BATCH, DEC_SEQ, D_MODEL), jnp.float32),
        "ffn1_norm": gain(ks[2], (L, D_MODEL)),
        "ffn1_w_gu": w(ks[3], (L, D_MODEL, 2 * D_FF), D_MODEL),
        "ffn1_w_down": w(ks[4], (L, D_FF, D_MODEL), D_FF),
        "mix_norm": gain(ks[5], (L, D_MODEL)),
        "w_in": w(ks[6], (L, D_MODEL, IN_COLS), D_MODEL),
        "mla_q_norm": gain(ks[7], (L, MLA_Q_RANK)),
        "mla_w_uq": w(ks[8], (L, MLA_Q_RANK, MLA_HEADS * (MLA_NOPE + MLA_ROPE)), MLA_Q_RANK),
        "mla_kv_norm": gain(ks[9], (L, MLA_KV_RANK)),
        "mla_w_ukv": w(ks[10], (L, MLA_KV_RANK, MLA_HEADS * (MLA_NOPE + MLA_V)), MLA_KV_RANK),
        "mla_w_o": w(ks[11], (L, MLA_HEADS * MLA_V, D_MODEL), MLA_HEADS * MLA_V),
        "conv_w": w(ks[12], (L, CONV_K, CONV_WIDTH), CONV_K),
        "conv_w_o": w(ks[13], (L, CONV_WIDTH, D_MODEL), CONV_WIDTH),
        "swa_sink": 0.5 * jax.random.normal(ks[14], (L, SWA_HEADS), jnp.float32),
        "swa_w_o": w(ks[15], (L, SWA_HEADS * SWA_HEAD_DIM, D_MODEL), SWA_HEADS * SWA_HEAD_DIM),
        "w_o": w(ks[16], (L, D_MODEL, D_MODEL), D_MODEL),
        "ffn2_norm": gain(ks[17], (L, D_MODEL)),
        "ffn2_w_gu": w(ks[18], (L, D_MODEL, 2 * D_FF), D_MODEL),
        "ffn2_w_down": w(ks[19], (L, D_FF, D_MODEL), D_FF),
        "final_norm": gain(ks[20], (D_MODEL,)),
    }


def reference(x_prompt, x_sample, ffn1_norm, ffn1_w_gu, ffn1_w_down, mix_norm, w_in,
              mla_q_norm, mla_w_uq, mla_kv_norm, mla_w_ukv, mla_w_o, conv_w, conv_w_o,
              swa_sink, swa_w_o, w_o, ffn2_norm, ffn2_w_gu, ffn2_w_down, final_norm):
    p = {
        'ffn1_norm': ffn1_norm, 'ffn1_w_gu': ffn1_w_gu, 'ffn1_w_down': ffn1_w_down,
        'mix_norm': mix_norm, 'w_in': w_in,
        'mla_q_norm': mla_q_norm, 'mla_w_uq': mla_w_uq, 'mla_kv_norm': mla_kv_norm,
        'mla_w_ukv': mla_w_ukv, 'mla_w_o': mla_w_o,
        'conv_w': conv_w, 'conv_w_o': conv_w_o,
        'swa_sink': swa_sink, 'swa_w_o': swa_w_o, 'w_o': w_o,
        'ffn2_norm': ffn2_norm, 'ffn2_w_gu': ffn2_w_gu, 'ffn2_w_down': ffn2_w_down,
    }
    y_prompt = encode(x_prompt, p, final_norm)
    y_sample = encode(x_sample, p, final_norm)
    return (y_prompt, y_sample)
```

```python
import functools

import jax
import jax.numpy as jnp
from jax import lax
from jax.experimental import pallas as pl
from jax.experimental.pallas import tpu as pltpu

D_MODEL = 1024
D_FF = 2816
NORM_EPS = 1e-6
ROPE_THETA = 10000.0

MLA_HEADS = 8
MLA_Q_RANK = 256
MLA_KV_RANK = 128
MLA_NOPE = 64
MLA_ROPE = 32
MLA_V = 64

CONV_WIDTH = 512

SWA_HEADS = 8
SWA_KV_HEADS = 2
SWA_HEAD_DIM = 64
SWA_WINDOW = 128
SWA_GROUP = SWA_HEADS // SWA_KV_HEADS

LANES = 128
HALF = LANES // 2
MLA_PAD = LANES
MASK_VALUE = -1e30

TOKEN_TILE = 512
FF_CHUNK = 1408
MLA_TQ = 512
MLA_TK = 512
SWA_TILE = 512
HALO = 8
VMEM_LIMIT = 56 * 1024 * 1024

C_QLAT = 0
C_KVLAT = C_QLAT + MLA_Q_RANK
C_CONV = C_KVLAT + MLA_KV_RANK
C_SQ = C_CONV + 3 * CONV_WIDTH
C_SQR = C_SQ + SWA_HEADS * SWA_HEAD_DIM
C_SK = C_SQR + SWA_HEADS * SWA_HEAD_DIM
C_SKR = C_SK + SWA_KV_HEADS * SWA_HEAD_DIM
C_SV = C_SKR + SWA_KV_HEADS * SWA_HEAD_DIM
C_KR = C_SV + SWA_KV_HEADS * SWA_HEAD_DIM
C_KRR = C_KR + LANES
C_MAIN = C_KRR + LANES

T_CQ, T_SQ, T_CK, T_SK, T_CS, T_SS = range(6)


def _bf16(x):
    return x.astype(jnp.bfloat16)


def _dot(a, b):
    return jnp.dot(a, b, preferred_element_type=jnp.float32)


def _dot_nt(a, b):
    return lax.dot_general(a, b, (((1,), (1,)), ((), ())), preferred_element_type=jnp.float32)


def _rmsnorm(x, g):
    return x * lax.rsqrt(jnp.mean(x * x, axis=-1, keepdims=True) + NORM_EPS) * g


def _sigmoid(x):
    return 1.0 / (1.0 + jnp.exp(-x))


def _const_spec(shape):
    return pl.BlockSpec(shape, lambda *_: (0,) * len(shape), pipeline_mode=pl.Buffered(1))


def _params(sem):
    return pltpu.CompilerParams(dimension_semantics=sem, vmem_limit_bytes=VMEM_LIMIT)


def _ffn_body(x_ref, g_ref, wgu_ref, wd_ref, fin_ref, o_ref, h_ref, *, final):
    x = x_ref[...]
    xn = _bf16(_rmsnorm(x, g_ref[...]))
    for c in range(D_FF // FF_CHUNK):
        a = _dot(xn, wgu_ref[:, c * FF_CHUNK:(c + 1) * FF_CHUNK])
        b = _dot(xn, wgu_ref[:, D_FF + c * FF_CHUNK:D_FF + (c + 1) * FF_CHUNK])
        h_ref[:, c * FF_CHUNK:(c + 1) * FF_CHUNK] = _bf16(a * _sigmoid(a) * b)
    y = x + 0.5 * _dot(h_ref[...], wd_ref[...])
    if final:
        y = _rmsnorm(y, fin_ref[...])
    o_ref[...] = y


def _ffn(x, g, wgu, wd, fin, final):
    t = x.shape[0]
    tok = pl.BlockSpec((TOKEN_TILE, D_MODEL), lambda i: (i, 0))
    return pl.pallas_call(
        functools.partial(_ffn_body, final=final),
        grid=(t // TOKEN_TILE,),
        in_specs=[tok, _const_spec((1, D_MODEL)), _const_spec((D_MODEL, 2 * D_FF)),
                  _const_spec((D_FF, D_MODEL)), _const_spec((1, D_MODEL))],
        out_specs=tok,
        out_shape=jax.ShapeDtypeStruct((t, D_MODEL), jnp.float32),
        scratch_shapes=[pltpu.VMEM((TOKEN_TILE, D_FF), jnp.bfloat16)],
        compiler_params=_params(("parallel",)),
        name="ffn_final" if final else "ffn",
    )(x, g, wgu, wd, fin)


def _proj_body(x_ref, xp_ref, xn_ref, tab_ref, g_ref, wm_ref, qn_ref, wq_ref, kvn_ref, wkv_ref,
               e_ref, cw_ref, q_ref, k_ref, v_ref, yb_ref, sq_ref, sk_ref, sv_ref, *, seq):
    i = pl.program_id(0)
    g = g_ref[...]
    u = _bf16(_rmsnorm(x_ref[...], g))

    def tab(j):
        return tab_ref[:, j * LANES:(j + 1) * LANES]

    def proj(c0, c1):
        return _dot(u, wm_ref[:, c0:c1])

    rq = _bf16(_rmsnorm(proj(C_QLAT, C_KVLAT), qn_ref[...]))
    qq = _dot(rq, wq_ref[...])
    for h in range(MLA_HEADS):
        blk = qq[:, h * MLA_PAD:(h + 1) * MLA_PAD]
        rot = qq[:, (MLA_HEADS + h) * MLA_PAD:(MLA_HEADS + h + 1) * MLA_PAD]
        q_ref[:, h * MLA_PAD:(h + 1) * MLA_PAD] = _bf16(blk * tab(T_CQ) + rot * tab(T_SQ))

    rkv = _bf16(_rmsnorm(proj(C_KVLAT, C_CONV), kvn_ref[...]))
    kvv = _dot(rkv, wkv_ref[...])
    kr = _bf16(proj(C_KR, C_KRR) * tab(T_CK) + proj(C_KRR, C_MAIN) * tab(T_SK))
    k_ref[...] = _bf16(kvv[:, :MLA_HEADS * MLA_PAD] + _dot(kr, e_ref[...]))
    v_ref[...] = _bf16(kvv[:, MLA_HEADS * MLA_PAD:])

    bcx = proj(C_CONV, C_SQ)
    z = bcx[:, CONV_WIDTH:2 * CONV_WIDTH] * bcx[:, 2 * CONV_WIDTH:]
    xh = jnp.concatenate([xp_ref[...], xn_ref[...]], axis=0)
    uh = _bf16(_rmsnorm(xh, g))
    cxh = _dot(uh, wm_ref[:, C_CONV + CONV_WIDTH:C_SQ])
    zh = cxh[:, :CONV_WIDTH] * cxh[:, CONV_WIDTH:]
    row0 = i * TOKEN_TILE
    has_prev = (row0 % seq) != 0
    has_next = ((row0 + TOKEN_TILE) % seq) != 0
    z_prev = jnp.where(has_prev, zh[HALO - 1:HALO, :], 0.0)
    z_next = jnp.where(has_next, zh[HALO:HALO + 1, :], 0.0)
    rows = lax.broadcasted_iota(jnp.int32, (TOKEN_TILE, CONV_WIDTH), 0)
    z_up = jnp.where(rows == 0, z_prev, pltpu.roll(z, 1, axis=0))
    z_dn = jnp.where(rows == TOKEN_TILE - 1, z_next, pltpu.roll(z, TOKEN_TILE - 1, axis=0))
    y = z_up * cw_ref[0:1, :] + z * cw_ref[1:2, :] + z_dn * cw_ref[2:3, :]
    yb_ref[...] = _bf16(bcx[:, :CONV_WIDTH] * y)

    sq = proj(C_SQ, C_SQR)
    sqr = proj(C_SQR, C_SK)
    scale = SWA_HEAD_DIM ** -0.5
    for p in range(SWA_HEADS * SWA_HEAD_DIM // LANES):
        sl = slice(p * LANES, (p + 1) * LANES)
        sq_ref[:, sl] = _bf16((sq[:, sl] * tab(T_CS) + sqr[:, sl] * tab(T_SS)) * scale)
    sk_ref[...] = _bf16(proj(C_SK, C_SKR) * tab(T_CS) + proj(C_SKR, C_SV) * tab(T_SS))
    sv_ref[...] = _bf16(proj(C_SV, C_KR))


def _proj(x, tables, lw, seq):
    t = x.shape[0]
    nt = t // TOKEN_TILE
    tiles_per_seq = seq // TOKEN_TILE
    halo_per_tile = TOKEN_TILE // HALO
    n_halo = t // HALO

    def tok(width):
        return pl.BlockSpec((TOKEN_TILE, width), lambda i: (i, 0))

    def out(width):
        return jax.ShapeDtypeStruct((t, width), jnp.bfloat16)

    in_specs = [
        tok(D_MODEL),
        pl.BlockSpec((HALO, D_MODEL), lambda i: (jnp.maximum(i * halo_per_tile - 1, 0), 0)),
        pl.BlockSpec((HALO, D_MODEL), lambda i: (jnp.minimum((i + 1) * halo_per_tile, n_halo - 1), 0)),
        pl.BlockSpec((TOKEN_TILE, 6 * LANES), lambda i: (i % tiles_per_seq, 0)),
        _const_spec((1, D_MODEL)),
        _const_spec((D_MODEL, C_MAIN)),
        _const_spec((1, MLA_Q_RANK)),
        _const_spec((MLA_Q_RANK, 2 * MLA_HEADS * MLA_PAD)),
        _const_spec((1, MLA_KV_RANK)),
        _const_spec((MLA_KV_RANK, MLA_HEADS * (MLA_PAD + MLA_V))),
        _const_spec((LANES, MLA_HEADS * MLA_PAD)),
        _const_spec((HALO, CONV_WIDTH)),
    ]
    widths = [MLA_HEADS * MLA_PAD, MLA_HEADS * MLA_PAD, MLA_HEADS * MLA_V, CONV_WIDTH,
              SWA_HEADS * SWA_HEAD_DIM, SWA_KV_HEADS * SWA_HEAD_DIM, SWA_KV_HEADS * SWA_HEAD_DIM]
    return pl.pallas_call(
        functools.partial(_proj_body, seq=seq),
        grid=(nt,),
        in_specs=in_specs,
        out_specs=[tok(w) for w in widths],
        out_shape=[out(w) for w in widths],
        compiler_params=_params(("parallel",)),
        name="proj",
    )(x, x, x, tables, lw["mix_norm"], lw["w_main"], lw["q_norm"], lw["w_q"], lw["kv_norm"],
      lw["w_kv"], lw["expand"], lw["conv_w"])


def _mla_body(q_ref, k_ref, v_ref, o_ref, *, seq):
    lane = lax.broadcasted_iota(jnp.int32, (MLA_TQ, LANES), 1)
    low = lane < HALF
    q0 = q_ref[:, :MLA_PAD]
    q1 = q_ref[:, MLA_PAD:]

    def step(j, carry):
        start = pl.multiple_of(j * MLA_TK, MLA_TK)
        vb = v_ref[pl.ds(start, MLA_TK), :]
        new = []
        for h, qh in enumerate((q0, q1)):
            m, l, acc = carry[h]
            kb = k_ref[pl.ds(start, MLA_TK), h * MLA_PAD:(h + 1) * MLA_PAD]
            s = _dot_nt(qh, kb)
            m_new = jnp.maximum(m, jnp.max(s, axis=-1, keepdims=True))
            alpha = jnp.exp(m - m_new)
            p = jnp.exp(s - m_new)
            l_new = alpha * l + jnp.sum(p, axis=-1, keepdims=True)
            acc_new = alpha * acc + _dot(_bf16(p), vb)
            new.append((m_new, l_new, acc_new))
        return tuple(new)

    init = tuple((jnp.full((MLA_TQ, 1), -jnp.inf, jnp.float32),
                  jnp.zeros((MLA_TQ, 1), jnp.float32),
                  jnp.zeros((MLA_TQ, LANES), jnp.float32)) for _ in range(2))
    (_, l0, a0), (_, l1, a1) = lax.fori_loop(0, seq // MLA_TK, step, init)
    o_ref[...] = _bf16(jnp.where(low, a0 / l0, a1 / l1))


def _mla(q, k, v, batch, seq):
    t = q.shape[0]
    nq = seq // MLA_TQ
    pairs = MLA_HEADS // 2
    return pl.pallas_call(
        functools.partial(_mla_body, seq=seq),
        grid=(batch, pairs, nq),
        in_specs=[
            pl.BlockSpec((MLA_TQ, 2 * MLA_PAD), lambda b, p, i: (b * nq + i, p)),
            pl.BlockSpec((seq, 2 * MLA_PAD), lambda b, p, i: (b, p)),
            pl.BlockSpec((seq, 2 * MLA_V), lambda b, p, i: (b, p)),
        ],
        out_specs=pl.BlockSpec((MLA_TQ, 2 * MLA_V), lambda b, p, i: (b * nq + i, p)),
        out_shape=jax.ShapeDtypeStruct((t, MLA_HEADS * MLA_V), jnp.bfloat16),
        compiler_params=_params(("parallel", "parallel", "arbitrary")),
        name="mla_attn",
    )(q, k, v)


def _swa_body(sink_ref, q_ref, kp_ref, kc_ref, kn_ref, vp_ref, vc_ref, vn_ref, o_ref,
              kbuf, vbuf, *, seq):
    w = SWA_WINDOW
    i = pl.program_id(0)
    kbuf[0:w, :] = kp_ref[...]
    kbuf[w:w + SWA_TILE, :] = kc_ref[...]
    kbuf[w + SWA_TILE:, :] = kn_ref[...]
    vbuf[0:w, :] = vp_ref[...]
    vbuf[w:w + SWA_TILE, :] = vc_ref[...]
    vbuf[w + SWA_TILE:, :] = vn_ref[...]

    rows = SWA_GROUP * w
    r = lax.broadcasted_iota(jnp.int32, (rows, 3 * w), 0) % w
    c = lax.broadcasted_iota(jnp.int32, (rows, 3 * w), 1)
    band = jnp.abs(c - w - r) <= SWA_WINDOW
    head_row = lax.broadcasted_iota(jnp.int32, (rows, 1), 0) // w
    lane_q = lax.broadcasted_iota(jnp.int32, (w, LANES), 1)
    pos0 = (i * SWA_TILE) % seq

    for n in range(SWA_TILE // w):
        key_pos = pos0 + (n - 1) * w + c
        mask = band & (key_pos >= 0) & (key_pos < seq)
        kb = kbuf[n * w:(n + 3) * w, :]
        vb = vbuf[n * w:(n + 3) * w, :]
        outs = [None] * SWA_HEADS
        for g in range(SWA_KV_HEADS):
            keep = (lane_q >= HALF) if g else (lane_q < HALF)
            parts = []
            sink = jnp.zeros((rows, 1), jnp.float32)
            for a in range(SWA_GROUP):
                head = g * SWA_GROUP + a
                blk = q_ref[n * w:(n + 1) * w, (head // 2) * LANES:(head // 2 + 1) * LANES]
                blk = blk.astype(jnp.float32)
                if head % 2 != g:
                    blk = pltpu.roll(blk, HALF, axis=1)
                parts.append(_bf16(jnp.where(keep, blk, 0.0)))
                sink = jnp.where(head_row == a, sink_ref[head], sink)
            qs = jnp.concatenate(parts, axis=0)
            s = jnp.where(mask, _dot_nt(qs, kb), MASK_VALUE)
            m = jnp.maximum(jnp.max(s, axis=-1, keepdims=True), sink)
            p = jnp.exp(s - m)
            denom = jnp.sum(p, axis=-1, keepdims=True) + jnp.exp(sink - m)
            pv = _dot(_bf16(p / denom), vb)
            for a in range(SWA_GROUP):
                head = g * SWA_GROUP + a
                blk = pv[a * w:(a + 1) * w, :]
                if head % 2 != g:
                    blk = pltpu.roll(blk, HALF, axis=1)
                outs[head] = blk
        for pr in range(SWA_HEADS // 2):
            o_ref[n * w:(n + 1) * w, pr * LANES:(pr + 1) * LANES] = _bf16(
                jnp.where(lane_q < HALF, outs[2 * pr], outs[2 * pr + 1]))


def _swa(sink, q, k, v, seq):
    t = q.shape[0]
    w = SWA_WINDOW
    per_tile = SWA_TILE // w
    n_blocks = t // w
    kvw = SWA_KV_HEADS * SWA_HEAD_DIM

    prev = pl.BlockSpec((w, kvw), lambda i: (jnp.maximum(i * per_tile - 1, 0), 0))
    cur = pl.BlockSpec((SWA_TILE, kvw), lambda i: (i, 0))
    nxt = pl.BlockSpec((w, kvw), lambda i: (jnp.minimum((i + 1) * per_tile, n_blocks - 1), 0))
    qspec = pl.BlockSpec((SWA_TILE, SWA_HEADS * SWA_HEAD_DIM), lambda i: (i, 0))
    return pl.pallas_call(
        functools.partial(_swa_body, seq=seq),
        grid=(t // SWA_TILE,),
        in_specs=[pl.BlockSpec(memory_space=pltpu.SMEM), qspec, prev, cur, nxt, prev, cur, nxt],
        out_specs=qspec,
        out_shape=jax.ShapeDtypeStruct((t, SWA_HEADS * SWA_HEAD_DIM), jnp.bfloat16),
        scratch_shapes=[pltpu.VMEM((SWA_TILE + 2 * w, kvw), jnp.bfloat16),
                        pltpu.VMEM((SWA_TILE + 2 * w, kvw), jnp.bfloat16)],
        compiler_params=_params(("parallel",)),
        name="swa_attn",
    )(sink, q, k, k, k, v, v, v)


def _merge_body(x_ref, oa_ref, yb_ref, oc_ref, g_ref, wg_ref, wa_ref, wb_ref, wc_ref, wo_ref, o_ref):
    x = x_ref[...]
    u = _bf16(_rmsnorm(x, g_ref[...]))
    branches = ((oa_ref, wa_ref), (yb_ref, wb_ref), (oc_ref, wc_ref))
    merged = None
    for b, (y_ref, w_ref) in enumerate(branches):
        gate = _sigmoid(_dot(u, wg_ref[:, b * D_MODEL:(b + 1) * D_MODEL]))
        term = gate * _dot(y_ref[...], w_ref[...])
        merged = term if merged is None else merged + term
    o_ref[...] = x + _dot(_bf16(merged), wo_ref[...])


def _merge(x, oa, yb, oc, lw):
    t = x.shape[0]

    def tok(width):
        return pl.BlockSpec((TOKEN_TILE, width), lambda i: (i, 0))

    half = MLA_HEADS * MLA_V
    return pl.pallas_call(
        _merge_body,
        grid=(t // TOKEN_TILE,),
        in_specs=[tok(D_MODEL), tok(half), tok(CONV_WIDTH), tok(SWA_HEADS * SWA_HEAD_DIM),
                  _const_spec((1, D_MODEL)), _const_spec((D_MODEL, 3 * D_MODEL)),
                  _const_spec((half, D_MODEL)), _const_spec((CONV_WIDTH, D_MODEL)),
                  _const_spec((SWA_HEADS * SWA_HEAD_DIM, D_MODEL)), _const_spec((D_MODEL, D_MODEL))],
        out_specs=tok(D_MODEL),
        out_shape=jax.ShapeDtypeStruct((t, D_MODEL), jnp.float32),
        compiler_params=_params(("parallel",)),
        name="merge",
    )(x, oa, yb, oc, lw["mix_norm"], lw["w_gate"], lw["w_a"], lw["w_b"], lw["w_c"], lw["w_o"])


def _rotate_half_cols(w, head_dim):
    lead = w.shape[:-1]
    wh = w.reshape(*lead, -1, 2, head_dim // 2)
    return jnp.concatenate([-wh[..., 1:2, :], wh[..., 0:1, :]], axis=-2).reshape(w.shape)


def _pad_cols(w, width):
    return jnp.pad(w, [(0, 0)] * (w.ndim - 1) + [(0, width - w.shape[-1])])


def _prepare_weights(w_in, mla_w_uq, mla_w_ukv):
    depth = w_in.shape[0]
    o = 0
    q_lat = w_in[..., o:o + MLA_Q_RANK]; o += MLA_Q_RANK
    kv_lat = w_in[..., o:o + MLA_KV_RANK]; o += MLA_KV_RANK
    k_rope = w_in[..., o:o + MLA_ROPE]; o += MLA_ROPE
    conv = w_in[..., o:o + 3 * CONV_WIDTH]; o += 3 * CONV_WIDTH
    sq = w_in[..., o:o + SWA_HEADS * SWA_HEAD_DIM]; o += SWA_HEADS * SWA_HEAD_DIM
    sk = w_in[..., o:o + SWA_KV_HEADS * SWA_HEAD_DIM]; o += SWA_KV_HEADS * SWA_HEAD_DIM
    sv = w_in[..., o:o + SWA_KV_HEADS * SWA_HEAD_DIM]; o += SWA_KV_HEADS * SWA_HEAD_DIM
    w_gate = w_in[..., o:]
    w_main = jnp.concatenate([
        q_lat, kv_lat, conv, sq, _rotate_half_cols(sq, SWA_HEAD_DIM), sk,
        _rotate_half_cols(sk, SWA_HEAD_DIM), sv, _pad_cols(k_rope, LANES),
        _pad_cols(_rotate_half_cols(k_rope, MLA_ROPE), LANES)], axis=-1)

    uq = mla_w_uq.reshape(depth, MLA_Q_RANK, MLA_HEADS, MLA_NOPE + MLA_ROPE)
    nope, rope = uq[..., :MLA_NOPE], uq[..., MLA_NOPE:]
    zeros = jnp.zeros_like
    pad = zeros(rope)
    w_q = jnp.concatenate([
        jnp.concatenate([nope, rope, pad], axis=-1).reshape(depth, MLA_Q_RANK, -1),
        jnp.concatenate([zeros(nope), _rotate_half_cols(rope, MLA_ROPE), pad], axis=-1)
        .reshape(depth, MLA_Q_RANK, -1)], axis=-1)

    ukv = mla_w_ukv.reshape(depth, MLA_KV_RANK, MLA_HEADS, MLA_NOPE + MLA_V)
    k_nope, v = ukv[..., :MLA_NOPE], ukv[..., MLA_NOPE:]
    w_kv = jnp.concatenate([
        jnp.concatenate([k_nope, zeros(k_nope)], axis=-1).reshape(depth, MLA_KV_RANK, -1),
        v.reshape(depth, MLA_KV_RANK, -1)], axis=-1)
    return _bf16(w_main), _bf16(w_gate), _bf16(w_q), _bf16(w_kv)


def _expand_matrix():
    src = jnp.arange(LANES)[:, None]
    dst = jnp.arange(MLA_HEADS * MLA_PAD)[None, :]
    hit = (src < MLA_ROPE) & (dst % MLA_PAD == MLA_NOPE + src)
    return hit.astype(jnp.bfloat16)


def _rope_tables(seq):
    def cos_sin(dim):
        inv = 1.0 / (ROPE_THETA ** (jnp.arange(0, dim, 2, dtype=jnp.float32) / dim))
        ang = jnp.arange(seq, dtype=jnp.float32)[:, None] * inv[None, :]
        return jnp.cos(ang), jnp.sin(ang)

    mc, ms = cos_sin(MLA_ROPE)
    sc, ss = cos_sin(SWA_HEAD_DIM)
    scale = (MLA_NOPE + MLA_ROPE) ** -0.5
    ones = jnp.ones((seq, MLA_NOPE), jnp.float32)
    zq = jnp.zeros((seq, MLA_PAD - MLA_NOPE - MLA_ROPE), jnp.float32)
    zn = jnp.zeros((seq, MLA_NOPE), jnp.float32)
    zk = jnp.zeros((seq, LANES - MLA_ROPE), jnp.float32)
    cols = [
        scale * jnp.concatenate([ones, mc, mc, zq], axis=1),
        scale * jnp.concatenate([zn, ms, ms, zq], axis=1),
        jnp.concatenate([mc, mc, zk], axis=1),
        jnp.concatenate([ms, ms, zk], axis=1),
        jnp.concatenate([sc, sc, sc, sc], axis=1),
        jnp.concatenate([ss, ss, ss, ss], axis=1),
    ]
    return jnp.concatenate(cols, axis=1)


def _encode(x, layers, final_norm):
    batch, seq, _ = x.shape
    tables = _rope_tables(seq)
    h = x.reshape(batch * seq, D_MODEL)
    for li, lw in enumerate(layers):
        h = _ffn(h, lw["ffn1_norm"], lw["ffn1_w_gu"], lw["ffn1_w_down"], final_norm, False)
        q, k, v, yb, sq, sk, sv = _proj(h, tables, lw, seq)
        oa = _mla(q, k, v, batch, seq)
        oc = _swa(lw["sink"], sq, sk, sv, seq)
        h = _merge(h, oa, yb, oc, lw)
        h = _ffn(h, lw["ffn2_norm"], lw["ffn2_w_gu"], lw["ffn2_w_down"], final_norm,
                 li == len(layers) - 1)
    return h.reshape(batch, seq, D_MODEL)


def kernel(x_prompt, x_sample, ffn1_norm, ffn1_w_gu, ffn1_w_down, mix_norm, w_in, mla_q_norm,
           mla_w_uq, mla_kv_norm, mla_w_ukv, mla_w_o, conv_w, conv_w_o, swa_sink, swa_w_o, w_o,
           ffn2_norm, ffn2_w_gu, ffn2_w_down, final_norm):
    depth = w_in.shape[0]
    w_main, w_gate, w_q, w_kv = _prepare_weights(w_in, mla_w_uq, mla_w_ukv)
    expand = _expand_matrix()
    conv_w_pad = jnp.pad(conv_w, ((0, 0), (0, HALO - conv_w.shape[1]), (0, 0)))
    stacked = {
        "ffn1_norm": ffn1_norm[:, None, :], "ffn1_w_gu": _bf16(ffn1_w_gu), "ffn1_w_down": _bf16(ffn1_w_down),
        "ffn2_norm": ffn2_norm[:, None, :], "ffn2_w_gu": _bf16(ffn2_w_gu), "ffn2_w_down": _bf16(ffn2_w_down),
        "mix_norm": mix_norm[:, None, :], "w_main": w_main, "w_gate": w_gate,
        "q_norm": mla_q_norm[:, None, :], "w_q": w_q, "kv_norm": mla_kv_norm[:, None, :], "w_kv": w_kv,
        "conv_w": conv_w_pad, "sink": swa_sink,
        "w_a": _bf16(mla_w_o), "w_b": _bf16(conv_w_o), "w_c": _bf16(swa_w_o), "w_o": _bf16(w_o),
    }
    layers = [{name: arr[l] for name, arr in stacked.items()} for l in range(depth)]
    for lw in layers:
        lw["expand"] = expand
    fin = final_norm[None, :]
    return (_encode(x_prompt, layers, fin), _encode(x_sample, layers, fin))
```

```python
import functools

import jax
import jax.numpy as jnp
from jax import lax
from jax.experimental import pallas as pl
from jax.experimental.pallas import tpu as pltpu

D_MODEL = 1024
D_FF = 2816
NORM_EPS = 1e-6
ROPE_THETA = 10000.0

MLA_HEADS = 8
MLA_Q_RANK = 256
MLA_KV_RANK = 128
MLA_NOPE = 64
MLA_ROPE = 32
MLA_V = 64

CONV_WIDTH = 512

SWA_HEADS = 8
SWA_KV_HEADS = 2
SWA_HEAD_DIM = 64
SWA_WINDOW = 128
SWA_GROUP = SWA_HEADS // SWA_KV_HEADS

LANES = 128
HALF = LANES // 2
MLA_PAD = LANES
MASK_VALUE = -1e30
LOG2_E = 1.4426950408889634

TOKEN_TILE = 512
FF_CHUNK = 1408
MLA_TQ = 256
MLA_TK = 2048
SWA_TILE = 512
HALO = 8
VMEM_LIMIT = 56 * 1024 * 1024

C_QLAT = 0
C_KVLAT = C_QLAT + MLA_Q_RANK
C_CONV = C_KVLAT + MLA_KV_RANK
C_SQ = C_CONV + 3 * CONV_WIDTH
C_SQR = C_SQ + SWA_HEADS * SWA_HEAD_DIM
C_SK = C_SQR + SWA_HEADS * SWA_HEAD_DIM
C_SKR = C_SK + SWA_KV_HEADS * SWA_HEAD_DIM
C_SV = C_SKR + SWA_KV_HEADS * SWA_HEAD_DIM
C_KR = C_SV + SWA_KV_HEADS * SWA_HEAD_DIM
C_KRR = C_KR + LANES
C_MAIN = C_KRR + LANES

T_CQ, T_SQ, T_CK, T_SK, T_CS, T_SS = range(6)


def _bf16(x):
    return x.astype(jnp.bfloat16)


def _dot(a, b):
    return jnp.dot(a, b, preferred_element_type=jnp.float32)


def _dot_nt(a, b):
    return lax.dot_general(a, b, (((1,), (1,)), ((), ())), preferred_element_type=jnp.float32)


def _rmsnorm(x, g):
    return x * lax.rsqrt(jnp.mean(x * x, axis=-1, keepdims=True) + NORM_EPS) * g


def _sigmoid(x):
    return 1.0 / (1.0 + jnp.exp(-x))


def _const_spec(shape):
    return pl.BlockSpec(shape, lambda *_: (0,) * len(shape), pipeline_mode=pl.Buffered(1))


def _params(sem):
    return pltpu.CompilerParams(dimension_semantics=sem, vmem_limit_bytes=VMEM_LIMIT)


def _ffn_body(x_ref, g_ref, wgu_ref, wd_ref, fin_ref, o_ref, h_ref, *, final):
    x = x_ref[...]
    xn = _bf16(_rmsnorm(x, g_ref[...]))
    for c in range(D_FF // FF_CHUNK):
        a = _dot(xn, wgu_ref[:, c * FF_CHUNK:(c + 1) * FF_CHUNK])
        b = _dot(xn, wgu_ref[:, D_FF + c * FF_CHUNK:D_FF + (c + 1) * FF_CHUNK])
        h_ref[:, c * FF_CHUNK:(c + 1) * FF_CHUNK] = _bf16(a * _sigmoid(a) * b)
    y = x + 0.5 * _dot(h_ref[...], wd_ref[...])
    if final:
        y = _rmsnorm(y, fin_ref[...])
    o_ref[...] = y


def _ffn(x, g, wgu, wd, fin, final):
    t = x.shape[0]
    tok = pl.BlockSpec((TOKEN_TILE, D_MODEL), lambda i: (i, 0))
    return pl.pallas_call(
        functools.partial(_ffn_body, final=final),
        grid=(t // TOKEN_TILE,),
        in_specs=[tok, _const_spec((1, D_MODEL)), _const_spec((D_MODEL, 2 * D_FF)),
                  _const_spec((D_FF, D_MODEL)), _const_spec((1, D_MODEL))],
        out_specs=tok,
        out_shape=jax.ShapeDtypeStruct((t, D_MODEL), jnp.float32),
        scratch_shapes=[pltpu.VMEM((TOKEN_TILE, D_FF), jnp.bfloat16)],
        compiler_params=_params(("parallel",)),
        name="ffn_final" if final else "ffn",
    )(x, g, wgu, wd, fin)


def _proj_body(x_ref, xp_ref, xn_ref, tab_ref, g_ref, wm_ref, qn_ref, wq_ref, kvn_ref, wkv_ref,
               e_ref, cw_ref, q_ref, k_ref, v_ref, yb_ref, sq_ref, sk_ref, sv_ref, *, seq):
    i = pl.program_id(0)
    g = g_ref[...]
    u = _bf16(_rmsnorm(x_ref[...], g))

    def tab(j):
        return tab_ref[:, j * LANES:(j + 1) * LANES]

    def proj(c0, c1):
        return _dot(u, wm_ref[:, c0:c1])

    rq = _bf16(_rmsnorm(proj(C_QLAT, C_KVLAT), qn_ref[...]))
    qq = _dot(rq, wq_ref[...])
    for h in range(MLA_HEADS):
        blk = qq[:, h * MLA_PAD:(h + 1) * MLA_PAD]
        rot = qq[:, (MLA_HEADS + h) * MLA_PAD:(MLA_HEADS + h + 1) * MLA_PAD]
        q_ref[:, h * MLA_PAD:(h + 1) * MLA_PAD] = _bf16(blk * tab(T_CQ) + rot * tab(T_SQ))

    rkv = _bf16(_rmsnorm(proj(C_KVLAT, C_CONV), kvn_ref[...]))
    kvv = _dot(rkv, wkv_ref[...])
    kr = _bf16(proj(C_KR, C_KRR) * tab(T_CK) + proj(C_KRR, C_MAIN) * tab(T_SK))
    k_ref[...] = _bf16(kvv[:, :MLA_HEADS * MLA_PAD] + _dot(kr, e_ref[...]))
    v_ref[...] = _bf16(kvv[:, MLA_HEADS * MLA_PAD:])

    bcx = proj(C_CONV, C_SQ)
    z = bcx[:, CONV_WIDTH:2 * CONV_WIDTH] * bcx[:, 2 * CONV_WIDTH:]
    xh = jnp.concatenate([xp_ref[...], xn_ref[...]], axis=0)
    uh = _bf16(_rmsnorm(xh, g))
    cxh = _dot(uh, wm_ref[:, C_CONV + CONV_WIDTH:C_SQ])
    zh = cxh[:, :CONV_WIDTH] * cxh[:, CONV_WIDTH:]
    row0 = i * TOKEN_TILE
    has_prev = (row0 % seq) != 0
    has_next = ((row0 + TOKEN_TILE) % seq) != 0
    z_prev = jnp.where(has_prev, zh[HALO - 1:HALO, :], 0.0)
    z_next = jnp.where(has_next, zh[HALO:HALO + 1, :], 0.0)
    rows = lax.broadcasted_iota(jnp.int32, (TOKEN_TILE, CONV_WIDTH), 0)
    z_up = jnp.where(rows == 0, z_prev, pltpu.roll(z, 1, axis=0))
    z_dn = jnp.where(rows == TOKEN_TILE - 1, z_next, pltpu.roll(z, TOKEN_TILE - 1, axis=0))
    y = z_up * cw_ref[0:1, :] + z * cw_ref[1:2, :] + z_dn * cw_ref[2:3, :]
    yb_ref[...] = _bf16(bcx[:, :CONV_WIDTH] * y)

    sq = proj(C_SQ, C_SQR)
    sqr = proj(C_SQR, C_SK)
    scale = SWA_HEAD_DIM ** -0.5
    for p in range(SWA_HEADS * SWA_HEAD_DIM // LANES):
        sl = slice(p * LANES, (p + 1) * LANES)
        sq_ref[:, sl] = _bf16((sq[:, sl] * tab(T_CS) + sqr[:, sl] * tab(T_SS)) * scale)
    sk_ref[...] = _bf16(proj(C_SK, C_SKR) * tab(T_CS) + proj(C_SKR, C_SV) * tab(T_SS))
    sv_ref[...] = _bf16(proj(C_SV, C_KR))


def _proj(x, tables, lw, seq):
    t = x.shape[0]
    nt = t // TOKEN_TILE
    tiles_per_seq = seq // TOKEN_TILE
    halo_per_tile = TOKEN_TILE // HALO
    n_halo = t // HALO

    def tok(width):
        return pl.BlockSpec((TOKEN_TILE, width), lambda i: (i, 0))

    def out(width):
        return jax.ShapeDtypeStruct((t, width), jnp.bfloat16)

    in_specs = [
        tok(D_MODEL),
        pl.BlockSpec((HALO, D_MODEL), lambda i: (jnp.maximum(i * halo_per_tile - 1, 0), 0)),
        pl.BlockSpec((HALO, D_MODEL), lambda i: (jnp.minimum((i + 1) * halo_per_tile, n_halo - 1), 0)),
        pl.BlockSpec((TOKEN_TILE, 6 * LANES), lambda i: (i % tiles_per_seq, 0)),
        _const_spec((1, D_MODEL)),
        _const_spec((D_MODEL, C_MAIN)),
        _const_spec((1, MLA_Q_RANK)),
        _const_spec((MLA_Q_RANK, 2 * MLA_HEADS * MLA_PAD)),
        _const_spec((1, MLA_KV_RANK)),
        _const_spec((MLA_KV_RANK, MLA_HEADS * (MLA_PAD + MLA_V))),
        _const_spec((LANES, MLA_HEADS * MLA_PAD)),
        _const_spec((HALO, CONV_WIDTH)),
    ]
    widths = [MLA_HEADS * MLA_PAD, MLA_HEADS * MLA_PAD, MLA_HEADS * MLA_V, CONV_WIDTH,
              SWA_HEADS * SWA_HEAD_DIM, SWA_KV_HEADS * SWA_HEAD_DIM, SWA_KV_HEADS * SWA_HEAD_DIM]
    return pl.pallas_call(
        functools.partial(_proj_body, seq=seq),
        grid=(nt,),
        in_specs=in_specs,
        out_specs=[tok(w) for w in widths],
        out_shape=[out(w) for w in widths],
        compiler_params=_params(("parallel",)),
        name="proj",
    )(x, x, x, tables, lw["mix_norm"], lw["w_main"], lw["q_norm"], lw["w_q"], lw["kv_norm"],
      lw["w_kv"], lw["expand"], lw["conv_w"])


def _mla_body(q_ref, k_ref, v_ref, o_ref, s_ref, m_ref, macc_ref, lacc_ref, acc_ref, *, seq):
    n_chunks = seq // MLA_TK
    tiles = MLA_TK // LANES
    low = lax.broadcasted_iota(jnp.int32, (MLA_TQ, LANES), 1) < HALF

    def stage(head_a, head_b):
        if head_a is not None:
            macc_ref[...] = jnp.full((MLA_TQ, LANES), -jnp.inf, jnp.float32)
        if head_b is not None:
            lacc_ref[...] = jnp.zeros((MLA_TQ, LANES), jnp.float32)
            acc_ref[...] = jnp.zeros((MLA_TQ, LANES), jnp.float32)

        def chunk(j, carry):
            start = pl.multiple_of(j * MLA_TK, MLA_TK)
            if head_a is not None:
                cols = slice(head_a * MLA_PAD, (head_a + 1) * MLA_PAD)
                s = _dot_nt(q_ref[:, cols], k_ref[pl.ds(start, MLA_TK), cols])
                s_ref[head_a % 2, j] = s
                mm = macc_ref[...]
                for i in range(tiles):
                    mm = jnp.maximum(mm, s[:, i * LANES:(i + 1) * LANES])
                macc_ref[...] = mm
            if head_b is not None:
                s = s_ref[head_b % 2, j]
                m = m_ref[head_b % 2]
                ll = lacc_ref[...]
                ps = []
                for i in range(tiles):
                    p = jnp.exp2(s[:, i * LANES:(i + 1) * LANES] - m)
                    ll = ll + p
                    ps.append(_bf16(p))
                lacc_ref[...] = ll
                pair = slice((head_b // 2) * LANES, (head_b // 2 + 1) * LANES)
                acc_ref[...] += _dot(jnp.concatenate(ps, axis=1), v_ref[pl.ds(start, MLA_TK), pair])
            return carry

        lax.fori_loop(0, n_chunks, chunk, 0)
        if head_a is not None:
            m = jnp.max(macc_ref[...], axis=-1, keepdims=True)
            m_ref[head_a % 2] = jnp.broadcast_to(m, (MLA_TQ, LANES))
        if head_b is not None:
            return acc_ref[...] / jnp.sum(lacc_ref[...], axis=-1, keepdims=True)
        return None

    o_even = None
    for t in range(MLA_HEADS + 1):
        head_b = t - 1 if t >= 1 else None
        o = stage(t if t < MLA_HEADS else None, head_b)
        if head_b is None:
            continue
        if head_b % 2 == 0:
            o_even = o
        else:
            pair = slice((head_b // 2) * LANES, (head_b // 2 + 1) * LANES)
            o_ref[:, pair] = _bf16(jnp.where(low, o_even, o))


def _mla(q, k, v, batch, seq):
    t = q.shape[0]
    nq = seq // MLA_TQ
    kv_spec = functools.partial(pl.BlockSpec, index_map=lambda b, i: (b, 0), pipeline_mode=pl.Buffered(1))
    return pl.pallas_call(
        functools.partial(_mla_body, seq=seq),
        grid=(batch, nq),
        in_specs=[
            pl.BlockSpec((MLA_TQ, MLA_HEADS * MLA_PAD), lambda b, i: (b * nq + i, 0)),
            kv_spec((seq, MLA_HEADS * MLA_PAD)),
            kv_spec((seq, MLA_HEADS * MLA_V)),
        ],
        out_specs=pl.BlockSpec((MLA_TQ, MLA_HEADS * MLA_V), lambda b, i: (b * nq + i, 0)),
        out_shape=jax.ShapeDtypeStruct((t, MLA_HEADS * MLA_V), jnp.bfloat16),
        scratch_shapes=[
            pltpu.VMEM((2, seq // MLA_TK, MLA_TQ, MLA_TK), jnp.float32),
            pltpu.VMEM((2, MLA_TQ, LANES), jnp.float32),
            pltpu.VMEM((MLA_TQ, LANES), jnp.float32),
            pltpu.VMEM((MLA_TQ, LANES), jnp.float32),
            pltpu.VMEM((MLA_TQ, LANES), jnp.float32),
        ],
        compiler_params=_params(("parallel", "arbitrary")),
        name="mla_attn",
    )(q, k, v)


def _swa_body(sink_ref, q_ref, kp_ref, kc_ref, kn_ref, vp_ref, vc_ref, vn_ref, o_ref,
              kbuf, vbuf, *, seq):
    w = SWA_WINDOW
    i = pl.program_id(0)
    kbuf[0:w, :] = kp_ref[...]
    kbuf[w:w + SWA_TILE, :] = kc_ref[...]
    kbuf[w + SWA_TILE:, :] = kn_ref[...]
    vbuf[0:w, :] = vp_ref[...]
    vbuf[w:w + SWA_TILE, :] = vc_ref[...]
    vbuf[w + SWA_TILE:, :] = vn_ref[...]

    rows = SWA_GROUP * w
    r = lax.broadcasted_iota(jnp.int32, (rows, 3 * w), 0) % w
    c = lax.broadcasted_iota(jnp.int32, (rows, 3 * w), 1)
    band = jnp.abs(c - w - r) <= SWA_WINDOW
    head_row = lax.broadcasted_iota(jnp.int32, (rows, 1), 0) // w
    lane_q = lax.broadcasted_iota(jnp.int32, (w, LANES), 1)
    pos0 = (i * SWA_TILE) % seq

    for n in range(SWA_TILE // w):
        key_pos = pos0 + (n - 1) * w + c
        mask = band & (key_pos >= 0) & (key_pos < seq)
        kb = kbuf[n * w:(n + 3) * w, :]
        vb = vbuf[n * w:(n + 3) * w, :]
        outs = [None] * SWA_HEADS
        for g in range(SWA_KV_HEADS):
            keep = (lane_q >= HALF) if g else (lane_q < HALF)
            parts = []
            sink = jnp.zeros((rows, 1), jnp.float32)
            for a in range(SWA_GROUP):
                head = g * SWA_GROUP + a
                blk = q_ref[n * w:(n + 1) * w, (head // 2) * LANES:(head // 2 + 1) * LANES]
                blk = blk.astype(jnp.float32)
                if head % 2 != g:
                    blk = pltpu.roll(blk, HALF, axis=1)
                parts.append(_bf16(jnp.where(keep, blk, 0.0)))
                sink = jnp.where(head_row == a, sink_ref[head], sink)
            qs = jnp.concatenate(parts, axis=0)
            s = jnp.where(mask, _dot_nt(qs, kb), MASK_VALUE)
            m = jnp.maximum(jnp.max(s, axis=-1, keepdims=True), sink)
            p = jnp.exp(s - m)
            denom = jnp.sum(p, axis=-1, keepdims=True) + jnp.exp(sink - m)
            pv = _dot(_bf16(p / denom), vb)
            for a in range(SWA_GROUP):
                head = g * SWA_GROUP + a
                blk = pv[a * w:(a + 1) * w, :]
                if head % 2 != g:
                    blk = pltpu.roll(blk, HALF, axis=1)
                outs[head] = blk
        for pr in range(SWA_HEADS // 2):
            o_ref[n * w:(n + 1) * w, pr * LANES:(pr + 1) * LANES] = _bf16(
                jnp.where(lane_q < HALF, outs[2 * pr], outs[2 * pr + 1]))


def _swa(sink, q, k, v, seq):
    t = q.shape[0]
    w = SWA_WINDOW
    per_tile = SWA_TILE // w
    n_blocks = t // w
    kvw = SWA_KV_HEADS * SWA_HEAD_DIM

    prev = pl.BlockSpec((w, kvw), lambda i: (jnp.maximum(i * per_tile - 1, 0), 0))
    cur = pl.BlockSpec((SWA_TILE, kvw), lambda i: (i, 0))
    nxt = pl.BlockSpec((w, kvw), lambda i: (jnp.minimum((i + 1) * per_tile, n_blocks - 1), 0))
    qspec = pl.BlockSpec((SWA_TILE, SWA_HEADS * SWA_HEAD_DIM), lambda i: (i, 0))
    return pl.pallas_call(
        functools.partial(_swa_body, seq=seq),
        grid=(t // SWA_TILE,),
        in_specs=[pl.BlockSpec(memory_space=pltpu.SMEM), qspec, prev, cur, nxt, prev, cur, nxt],
        out_specs=qspec,
        out_shape=jax.ShapeDtypeStruct((t, SWA_HEADS * SWA_HEAD_DIM), jnp.bfloat16),
        scratch_shapes=[pltpu.VMEM((SWA_TILE + 2 * w, kvw), jnp.bfloat16),
                        pltpu.VMEM((SWA_TILE + 2 * w, kvw), jnp.bfloat16)],
        compiler_params=_params(("parallel",)),
        name="swa_attn",
    )(sink, q, k, k, k, v, v, v)


def _merge_body(x_ref, oa_ref, yb_ref, oc_ref, g_ref, wg_ref, wa_ref, wb_ref, wc_ref, wo_ref, o_ref):
    x = x_ref[...]
    u = _bf16(_rmsnorm(x, g_ref[...]))
    branches = ((oa_ref, wa_ref), (yb_ref, wb_ref), (oc_ref, wc_ref))
    merged = None
    for b, (y_ref, w_ref) in enumerate(branches):
        gate = _sigmoid(_dot(u, wg_ref[:, b * D_MODEL:(b + 1) * D_MODEL]))
        term = gate * _dot(y_ref[...], w_ref[...])
        merged = term if merged is None else merged + term
    o_ref[...] = x + _dot(_bf16(merged), wo_ref[...])


def _merge(x, oa, yb, oc, lw):
    t = x.shape[0]

    def tok(width):
        return pl.BlockSpec((TOKEN_TILE, width), lambda i: (i, 0))

    half = MLA_HEADS * MLA_V
    return pl.pallas_call(
        _merge_body,
        grid=(t // TOKEN_TILE,),
        in_specs=[tok(D_MODEL), tok(half), tok(CONV_WIDTH), tok(SWA_HEADS * SWA_HEAD_DIM),
                  _const_spec((1, D_MODEL)), _const_spec((D_MODEL, 3 * D_MODEL)),
                  _const_spec((half, D_MODEL)), _const_spec((CONV_WIDTH, D_MODEL)),
                  _const_spec((SWA_HEADS * SWA_HEAD_DIM, D_MODEL)), _const_spec((D_MODEL, D_MODEL))],
        out_specs=tok(D_MODEL),
        out_shape=jax.ShapeDtypeStruct((t, D_MODEL), jnp.float32),
        compiler_params=_params(("parallel",)),
        name="merge",
    )(x, oa, yb, oc, lw["mix_norm"], lw["w_gate"], lw["w_a"], lw["w_b"], lw["w_c"], lw["w_o"])


def _rotate_half_cols(w, head_dim):
    lead = w.shape[:-1]
    wh = w.reshape(*lead, -1, 2, head_dim // 2)
    return jnp.concatenate([-wh[..., 1:2, :], wh[..., 0:1, :]], axis=-2).reshape(w.shape)


def _pad_cols(w, width):
    return jnp.pad(w, [(0, 0)] * (w.ndim - 1) + [(0, width - w.shape[-1])])


def _prepare_weights(w_in, mla_w_uq, mla_w_ukv):
    depth = w_in.shape[0]
    o = 0
    q_lat = w_in[..., o:o + MLA_Q_RANK]; o += MLA_Q_RANK
    kv_lat = w_in[..., o:o + MLA_KV_RANK]; o += MLA_KV_RANK
    k_rope = w_in[..., o:o + MLA_ROPE]; o += MLA_ROPE
    conv = w_in[..., o:o + 3 * CONV_WIDTH]; o += 3 * CONV_WIDTH
    sq = w_in[..., o:o + SWA_HEADS * SWA_HEAD_DIM]; o += SWA_HEADS * SWA_HEAD_DIM
    sk = w_in[..., o:o + SWA_KV_HEADS * SWA_HEAD_DIM]; o += SWA_KV_HEADS * SWA_HEAD_DIM
    sv = w_in[..., o:o + SWA_KV_HEADS * SWA_HEAD_DIM]; o += SWA_KV_HEADS * SWA_HEAD_DIM
    w_gate = w_in[..., o:]
    w_main = jnp.concatenate([
        q_lat, kv_lat, conv, sq, _rotate_half_cols(sq, SWA_HEAD_DIM), sk,
        _rotate_half_cols(sk, SWA_HEAD_DIM), sv, _pad_cols(k_rope, LANES),
        _pad_cols(_rotate_half_cols(k_rope, MLA_ROPE), LANES)], axis=-1)

    uq = mla_w_uq.reshape(depth, MLA_Q_RANK, MLA_HEADS, MLA_NOPE + MLA_ROPE)
    nope, rope = uq[..., :MLA_NOPE], uq[..., MLA_NOPE:]
    zeros = jnp.zeros_like
    pad = zeros(rope)
    w_q = jnp.concatenate([
        jnp.concatenate([nope, rope, pad], axis=-1).reshape(depth, MLA_Q_RANK, -1),
        jnp.concatenate([zeros(nope), _rotate_half_cols(rope, MLA_ROPE), pad], axis=-1)
        .reshape(depth, MLA_Q_RANK, -1)], axis=-1)

    ukv = mla_w_ukv.reshape(depth, MLA_KV_RANK, MLA_HEADS, MLA_NOPE + MLA_V)
    k_nope, v = ukv[..., :MLA_NOPE], ukv[..., MLA_NOPE:]
    w_kv = jnp.concatenate([
        jnp.concatenate([k_nope, zeros(k_nope)], axis=-1).reshape(depth, MLA_KV_RANK, -1),
        v.reshape(depth, MLA_KV_RANK, -1)], axis=-1)
    return _bf16(w_main), _bf16(w_gate), _bf16(w_q), _bf16(w_kv)


def _expand_matrix():
    src = jnp.arange(LANES)[:, None]
    dst = jnp.arange(MLA_HEADS * MLA_PAD)[None, :]
    hit = (src < MLA_ROPE) & (dst % MLA_PAD == MLA_NOPE + src)
    return hit.astype(jnp.bfloat16)


def _rope_tables(seq):
    def cos_sin(dim):
        inv = 1.0 / (ROPE_THETA ** (jnp.arange(0, dim, 2, dtype=jnp.float32) / dim))
        ang = jnp.arange(seq, dtype=jnp.float32)[:, None] * inv[None, :]
        return jnp.cos(ang), jnp.sin(ang)

    mc, ms = cos_sin(MLA_ROPE)
    sc, ss = cos_sin(SWA_HEAD_DIM)
    scale = (MLA_NOPE + MLA_ROPE) ** -0.5 * LOG2_E
    ones = jnp.ones((seq, MLA_NOPE), jnp.float32)
    zq = jnp.zeros((seq, MLA_PAD - MLA_NOPE - MLA_ROPE), jnp.float32)
    zn = jnp.zeros((seq, MLA_NOPE), jnp.float32)
    zk = jnp.zeros((seq, LANES - MLA_ROPE), jnp.float32)
    cols = [
        scale * jnp.concatenate([ones, mc, mc, zq], axis=1),
        scale * jnp.concatenate([zn, ms, ms, zq], axis=1),
        jnp.concatenate([mc, mc, zk], axis=1),
        jnp.concatenate([ms, ms, zk], axis=1),
        jnp.concatenate([sc, sc, sc, sc], axis=1),
        jnp.concatenate([ss, ss, ss, ss], axis=1),
    ]
    return jnp.concatenate(cols, axis=1)


def _encode(x, layers, final_norm):
    batch, seq, _ = x.shape
    tables = _rope_tables(seq)
    h = x.reshape(batch * seq, D_MODEL)
    for li, lw in enumerate(layers):
        h = _ffn(h, lw["ffn1_norm"], lw["ffn1_w_gu"], lw["ffn1_w_down"], final_norm, False)
        q, k, v, yb, sq, sk, sv = _proj(h, tables, lw, seq)
        oa = _mla(q, k, v, batch, seq)
        oc = _swa(lw["sink"], sq, sk, sv, seq)
        h = _merge(h, oa, yb, oc, lw)
        h = _ffn(h, lw["ffn2_norm"], lw["ffn2_w_gu"], lw["ffn2_w_down"], final_norm,
                 li == len(layers) - 1)
    return h.reshape(batch, seq, D_MODEL)


def kernel(x_prompt, x_sample, ffn1_norm, ffn1_w_gu, ffn1_w_down, mix_norm, w_in, mla_q_norm,
           mla_w_uq, mla_kv_norm, mla_w_ukv, mla_w_o, conv_w, conv_w_o, swa_sink, swa_w_o, w_o,
           ffn2_norm, ffn2_w_gu, ffn2_w_down, final_norm):
    depth = w_in.shape[0]
    w_main, w_gate, w_q, w_kv = _prepare_weights(w_in, mla_w_uq, mla_w_ukv)
    expand = _expand_matrix()
    conv_w_pad = jnp.pad(conv_w, ((0, 0), (0, HALO - conv_w.shape[1]), (0, 0)))
    stacked = {
        "ffn1_norm": ffn1_norm[:, None, :], "ffn1_w_gu": _bf16(ffn1_w_gu), "ffn1_w_down": _bf16(ffn1_w_down),
        "ffn2_norm": ffn2_norm[:, None, :], "ffn2_w_gu": _bf16(ffn2_w_gu), "ffn2_w_down": _bf16(ffn2_w_down),
        "mix_norm": mix_norm[:, None, :], "w_main": w_main, "w_gate": w_gate,
        "q_norm": mla_q_norm[:, None, :], "w_q": w_q, "kv_norm": mla_kv_norm[:, None, :], "w_kv": w_kv,
        "conv_w": conv_w_pad, "sink": swa_sink,
        "w_a": _bf16(mla_w_o), "w_b": _bf16(conv_w_o), "w_c": _bf16(swa_w_o), "w_o": _bf16(w_o),
    }
    layers = [{name: arr[l] for name, arr in stacked.items()} for l in range(depth)]
    for lw in layers:
        lw["expand"] = expand
    fin = final_norm[None, :]
    return (_encode(x_prompt, layers, fin), _encode(x_sample, layers, fin))
```

```python
import functools

import jax
import jax.numpy as jnp
from jax import lax
from jax.experimental import pallas as pl
from jax.experimental.pallas import tpu as pltpu

D_MODEL = 1024
D_FF = 2816
NORM_EPS = 1e-6
ROPE_THETA = 10000.0

MLA_HEADS = 8
MLA_Q_RANK = 256
MLA_KV_RANK = 128
MLA_NOPE = 64
MLA_ROPE = 32
MLA_V = 64

CONV_WIDTH = 512

SWA_HEADS = 8
SWA_KV_HEADS = 2
SWA_HEAD_DIM = 64
SWA_WINDOW = 128
SWA_GROUP = SWA_HEADS // SWA_KV_HEADS

LANES = 128
HALF = LANES // 2
MLA_PAD = LANES
MASK_VALUE = -1e30
LOG2_E = 1.4426950408889634

TOKEN_TILE = 512
FF_CHUNK = 1408
MLA_TQ = 256
MLA_TK = 2048
MLA_QBLOCK = 2048
MLA_GROUP = 4
SWA_TILE = 512
HALO = 8
VMEM_LIMIT = 56 * 1024 * 1024

C_QLAT = 0
C_KVLAT = C_QLAT + MLA_Q_RANK
C_CONV = C_KVLAT + MLA_KV_RANK
C_SQ = C_CONV + 3 * CONV_WIDTH
C_SQR = C_SQ + SWA_HEADS * SWA_HEAD_DIM
C_SK = C_SQR + SWA_HEADS * SWA_HEAD_DIM
C_SKR = C_SK + SWA_KV_HEADS * SWA_HEAD_DIM
C_SV = C_SKR + SWA_KV_HEADS * SWA_HEAD_DIM
C_KR = C_SV + SWA_KV_HEADS * SWA_HEAD_DIM
C_KRR = C_KR + LANES
C_MAIN = C_KRR + LANES

T_CQ, T_SQ, T_CK, T_SK, T_CS, T_SS = range(6)


def _bf16(x):
    return x.astype(jnp.bfloat16)


def _dot(a, b):
    return jnp.dot(a, b, preferred_element_type=jnp.float32)


def _dot_nt(a, b):
    return lax.dot_general(a, b, (((1,), (1,)), ((), ())), preferred_element_type=jnp.float32)


def _rmsnorm(x, g):
    return x * lax.rsqrt(jnp.mean(x * x, axis=-1, keepdims=True) + NORM_EPS) * g


def _sigmoid(x):
    return 1.0 / (1.0 + jnp.exp(-x))


def _const_spec(shape):
    return pl.BlockSpec(shape, lambda *_: (0,) * len(shape), pipeline_mode=pl.Buffered(1))


def _layer_spec(shape, layer):
    return pl.BlockSpec((None,) + shape, lambda *_: (layer,) + (0,) * len(shape),
                        pipeline_mode=pl.Buffered(1))


def _params(sem):
    return pltpu.CompilerParams(dimension_semantics=sem, vmem_limit_bytes=VMEM_LIMIT)


def _ffn_body(x_ref, g_ref, wgu_ref, wd_ref, fin_ref, o_ref, h_ref, *, final):
    x = x_ref[...]
    xn = _bf16(_rmsnorm(x, g_ref[...]))
    for c in range(D_FF // FF_CHUNK):
        a = _dot(xn, wgu_ref[:, c * FF_CHUNK:(c + 1) * FF_CHUNK])
        b = _dot(xn, wgu_ref[:, D_FF + c * FF_CHUNK:D_FF + (c + 1) * FF_CHUNK])
        h_ref[:, c * FF_CHUNK:(c + 1) * FF_CHUNK] = _bf16(a * _sigmoid(a) * b)
    y = x + 0.5 * _dot(h_ref[...], wd_ref[...])
    if final:
        y = _rmsnorm(y, fin_ref[...])
    o_ref[...] = y


def _ffn(x, g, wgu, wd, fin, layer, final):
    t = x.shape[0]
    tok = pl.BlockSpec((TOKEN_TILE, D_MODEL), lambda i: (i, 0))
    return pl.pallas_call(
        functools.partial(_ffn_body, final=final),
        grid=(t // TOKEN_TILE,),
        in_specs=[tok, _layer_spec((1, D_MODEL), layer), _layer_spec((D_MODEL, 2 * D_FF), layer),
                  _layer_spec((D_FF, D_MODEL), layer), _const_spec((1, D_MODEL))],
        out_specs=tok,
        out_shape=jax.ShapeDtypeStruct((t, D_MODEL), jnp.float32),
        scratch_shapes=[pltpu.VMEM((TOKEN_TILE, D_FF), jnp.bfloat16)],
        compiler_params=_params(("parallel",)),
        name="ffn_final" if final else "ffn",
    )(x, g, wgu, wd, fin)


def _proj_body(x_ref, xp_ref, xn_ref, tab_ref, g_ref, wm_ref, qn_ref, wq_ref, kvn_ref, wkv_ref,
               e_ref, cw_ref, q_ref, k_ref, v_ref, yb_ref, sq_ref, sk_ref, sv_ref, *, seq):
    i = pl.program_id(0)
    g = g_ref[...]
    u = _bf16(_rmsnorm(x_ref[...], g))

    def tab(j):
        return tab_ref[:, j * LANES:(j + 1) * LANES]

    def proj(c0, c1):
        return _dot(u, wm_ref[:, c0:c1])

    rq = _bf16(_rmsnorm(proj(C_QLAT, C_KVLAT), qn_ref[...]))
    qq = _dot(rq, wq_ref[...])
    for h in range(MLA_HEADS):
        blk = qq[:, h * MLA_PAD:(h + 1) * MLA_PAD]
        rot = qq[:, (MLA_HEADS + h) * MLA_PAD:(MLA_HEADS + h + 1) * MLA_PAD]
        q_ref[h] = _bf16(blk * tab(T_CQ) + rot * tab(T_SQ))

    rkv = _bf16(_rmsnorm(proj(C_KVLAT, C_CONV), kvn_ref[...]))
    kvv = _dot(rkv, wkv_ref[...])
    kr = _bf16(proj(C_KR, C_KRR) * tab(T_CK) + proj(C_KRR, C_MAIN) * tab(T_SK))
    kre = _dot(kr, e_ref[...])
    lane = lax.broadcasted_iota(jnp.int32, (1, LANES), 1)
    for h in range(MLA_HEADS):
        sl = slice(h * MLA_PAD, (h + 1) * MLA_PAD)
        k_ref[h] = _bf16(kvv[:, sl] + kre[:, sl])
        ones = jnp.where((lane >= HALF) == (h % 2 == 0), 1.0, 0.0)
        v_ref[h] = _bf16(kvv[:, MLA_HEADS * MLA_PAD + h * LANES:MLA_HEADS * MLA_PAD + (h + 1) * LANES] + ones)

    bcx = proj(C_CONV, C_SQ)
    z = bcx[:, CONV_WIDTH:2 * CONV_WIDTH] * bcx[:, 2 * CONV_WIDTH:]
    xh = jnp.concatenate([xp_ref[...], xn_ref[...]], axis=0)
    uh = _bf16(_rmsnorm(xh, g))
    cxh = _dot(uh, wm_ref[:, C_CONV + CONV_WIDTH:C_SQ])
    zh = cxh[:, :CONV_WIDTH] * cxh[:, CONV_WIDTH:]
    row0 = i * TOKEN_TILE
    has_prev = (row0 % seq) != 0
    has_next = ((row0 + TOKEN_TILE) % seq) != 0
    z_prev = jnp.where(has_prev, zh[HALO - 1:HALO, :], 0.0)
    z_next = jnp.where(has_next, zh[HALO:HALO + 1, :], 0.0)
    rows = lax.broadcasted_iota(jnp.int32, (TOKEN_TILE, CONV_WIDTH), 0)
    z_up = jnp.where(rows == 0, z_prev, pltpu.roll(z, 1, axis=0))
    z_dn = jnp.where(rows == TOKEN_TILE - 1, z_next, pltpu.roll(z, TOKEN_TILE - 1, axis=0))
    y = z_up * cw_ref[0:1, :] + z * cw_ref[1:2, :] + z_dn * cw_ref[2:3, :]
    yb_ref[...] = _bf16(bcx[:, :CONV_WIDTH] * y)

    sq = proj(C_SQ, C_SQR)
    sqr = proj(C_SQR, C_SK)
    scale = SWA_HEAD_DIM ** -0.5 * LOG2_E
    for p in range(SWA_HEADS * SWA_HEAD_DIM // LANES):
        sl = slice(p * LANES, (p + 1) * LANES)
        sq_ref[:, sl] = _bf16((sq[:, sl] * tab(T_CS) + sqr[:, sl] * tab(T_SS)) * scale)
    sk_ref[...] = _bf16(proj(C_SK, C_SKR) * tab(T_CS) + proj(C_SKR, C_SV) * tab(T_SS))
    sv_ref[...] = _bf16(proj(C_SV, C_KR))


def _proj(x, tables, lw, layer, seq):
    t = x.shape[0]
    nt = t // TOKEN_TILE
    tiles_per_seq = seq // TOKEN_TILE
    halo_per_tile = TOKEN_TILE // HALO
    n_halo = t // HALO

    def tok(width):
        return pl.BlockSpec((TOKEN_TILE, width), lambda i: (i, 0))

    def out(width):
        return jax.ShapeDtypeStruct((t, width), jnp.bfloat16)

    in_specs = [
        tok(D_MODEL),
        pl.BlockSpec((HALO, D_MODEL), lambda i: (jnp.maximum(i * halo_per_tile - 1, 0), 0)),
        pl.BlockSpec((HALO, D_MODEL), lambda i: (jnp.minimum((i + 1) * halo_per_tile, n_halo - 1), 0)),
        pl.BlockSpec((TOKEN_TILE, 6 * LANES), lambda i: (i % tiles_per_seq, 0)),
        _layer_spec((1, D_MODEL), layer),
        _layer_spec((D_MODEL, C_MAIN), layer),
        _layer_spec((1, MLA_Q_RANK), layer),
        _layer_spec((MLA_Q_RANK, 2 * MLA_HEADS * MLA_PAD), layer),
        _layer_spec((1, MLA_KV_RANK), layer),
        _layer_spec((MLA_KV_RANK, MLA_HEADS * (MLA_PAD + LANES)), layer),
        _const_spec((LANES, MLA_HEADS * MLA_PAD)),
        _layer_spec((HALO, CONV_WIDTH), layer),
    ]
    widths = [CONV_WIDTH, SWA_HEADS * SWA_HEAD_DIM, SWA_KV_HEADS * SWA_HEAD_DIM,
              SWA_KV_HEADS * SWA_HEAD_DIM]
    head_spec = pl.BlockSpec((MLA_HEADS, TOKEN_TILE, LANES), lambda i: (0, i, 0))
    head_shape = jax.ShapeDtypeStruct((MLA_HEADS, t, LANES), jnp.bfloat16)
    return pl.pallas_call(
        functools.partial(_proj_body, seq=seq),
        grid=(nt,),
        in_specs=in_specs,
        out_specs=[head_spec] * 3 + [tok(w) for w in widths],
        out_shape=[head_shape] * 3 + [out(w) for w in widths],
        compiler_params=_params(("parallel",)),
        name="proj",
    )(x, x, x, tables, lw["mix_norm"], lw["w_main"], lw["q_norm"], lw["w_q"], lw["kv_norm"],
      lw["w_kv"], lw["expand"], lw["conv_w"])


def _mla_body(q_ref, k_ref, v_ref, o_ref, s_ref, m_ref, o_scr, *, seq):
    n_chunks = seq // MLA_TK
    n_items = (MLA_QBLOCK // MLA_TQ) * MLA_GROUP
    low = lax.broadcasted_iota(jnp.int32, (MLA_TQ, LANES), 1) < HALF

    def rows_of(t):
        return pl.ds(pl.multiple_of((t // MLA_GROUP) * MLA_TQ, MLA_TQ), MLA_TQ)

    def pass_a(t):
        head = t % MLA_GROUP
        q = q_ref[head, rows_of(t), :]
        mm = None
        for c in range(n_chunks):
            s = _dot_nt(q, k_ref[head, c * MLA_TK:(c + 1) * MLA_TK, :])
            s_ref[t % 2, c] = s
            cm = jnp.max(s, axis=-1, keepdims=True)
            mm = cm if mm is None else jnp.maximum(mm, cm)
        m_ref[t % 2] = jnp.broadcast_to(mm, (MLA_TQ, LANES))

    def pass_b(t):
        head = t % MLA_GROUP
        m = m_ref[t % 2][:, :1]
        res = None
        for c in range(n_chunks):
            p = _bf16(jnp.exp2(s_ref[t % 2, c] - m))
            pv = _dot(p, v_ref[head, c * MLA_TK:(c + 1) * MLA_TK, :])
            res = pv if res is None else res + pv
        val = res / pltpu.roll(res, HALF, axis=1)
        odd = (head % 2) == 1
        valid = jnp.logical_xor(low, odd)
        new = jnp.where(valid, val, jnp.where(odd, o_scr[...], 0.0))
        o_scr[...] = new
        o_ref[head // 2, rows_of(t), :] = _bf16(new)

    o_scr[...] = jnp.zeros((MLA_TQ, LANES), jnp.float32)
    pass_a(0)

    def item(t, carry):
        pass_a(t)
        pass_b(t - 1)
        return carry

    lax.fori_loop(1, n_items, item, 0)
    pass_b(n_items - 1)


def _mla(q, k, v, batch, seq):
    t = q.shape[1]
    nqb = seq // MLA_QBLOCK
    groups = MLA_HEADS // MLA_GROUP
    kv_spec = pl.BlockSpec((MLA_GROUP, seq, LANES), lambda b, g, i: (g, b, 0), pipeline_mode=pl.Buffered(1))
    return pl.pallas_call(
        functools.partial(_mla_body, seq=seq),
        grid=(batch, groups, nqb),
        in_specs=[pl.BlockSpec((MLA_GROUP, MLA_QBLOCK, LANES), lambda b, g, i: (g, b * nqb + i, 0)),
                  kv_spec, kv_spec],
        out_specs=pl.BlockSpec((MLA_GROUP // 2, MLA_QBLOCK, LANES), lambda b, g, i: (g, b * nqb + i, 0)),
        out_shape=jax.ShapeDtypeStruct((MLA_HEADS // 2, t, LANES), jnp.bfloat16),
        scratch_shapes=[
            pltpu.VMEM((2, seq // MLA_TK, MLA_TQ, MLA_TK), jnp.float32),
            pltpu.VMEM((2, MLA_TQ, LANES), jnp.float32),
            pltpu.VMEM((MLA_TQ, LANES), jnp.float32),
        ],
        compiler_params=_params(("parallel", "parallel", "arbitrary")),
        name="mla_attn",
    )(q, k, v)


def _swa_body(sink_ref, bias_ref, q_ref, kp_ref, kc_ref, kn_ref, vp_ref, vc_ref, vn_ref, o_ref,
              kbuf, vbuf, *, seq):
    w = SWA_WINDOW
    n_sub = SWA_TILE // w
    i = pl.program_id(0)
    kbuf[0:w, :] = kp_ref[...]
    kbuf[w:w + SWA_TILE, :] = kc_ref[...]
    kbuf[w + SWA_TILE:, :] = kn_ref[...]
    vbuf[0:w, :] = vp_ref[...]
    vbuf[w:w + SWA_TILE, :] = vc_ref[...]
    vbuf[w + SWA_TILE:, :] = vn_ref[...]

    rows = SWA_GROUP * w
    head_row = lax.broadcasted_iota(jnp.int32, (rows, 1), 0) // w
    lane_q = lax.broadcasted_iota(jnp.int32, (w, LANES), 1)
    pos0 = (i * SWA_TILE) % seq
    first = jnp.where(pos0 == 0, 1, 0)
    last = jnp.where(pos0 + SWA_TILE == seq, 2, 0)

    for n in range(n_sub):
        bias = bias_ref[first if n == 0 else (last if n == n_sub - 1 else 0)]
        kb = kbuf[n * w:(n + 3) * w, :]
        vb = vbuf[n * w:(n + 3) * w, :]
        outs = [None] * SWA_HEADS
        for g in range(SWA_KV_HEADS):
            keep = (lane_q >= HALF) if g else (lane_q < HALF)
            parts = []
            sink = jnp.zeros((rows, 1), jnp.float32)
            for a in range(SWA_GROUP):
                head = g * SWA_GROUP + a
                blk = q_ref[n * w:(n + 1) * w, (head // 2) * LANES:(head // 2 + 1) * LANES]
                blk = blk.astype(jnp.float32)
                if head % 2 != g:
                    blk = pltpu.roll(blk, HALF, axis=1)
                parts.append(_bf16(jnp.where(keep, blk, 0.0)))
                sink = jnp.where(head_row == a, sink_ref[head] * LOG2_E, sink)
            qs = jnp.concatenate(parts, axis=0)
            s = jnp.minimum(_dot_nt(qs, kb), bias)
            m = jnp.maximum(jnp.max(s, axis=-1, keepdims=True), sink)
            p = jnp.exp2(s - m)
            denom = jnp.sum(p, axis=-1, keepdims=True) + jnp.exp2(sink - m)
            pv = _dot(_bf16(p), vb) * (1.0 / denom)
            for a in range(SWA_GROUP):
                head = g * SWA_GROUP + a
                blk = pv[a * w:(a + 1) * w, :]
                if head % 2 != g:
                    blk = pltpu.roll(blk, HALF, axis=1)
                outs[head] = blk
        for pr in range(SWA_HEADS // 2):
            o_ref[n * w:(n + 1) * w, pr * LANES:(pr + 1) * LANES] = _bf16(
                jnp.where(lane_q < HALF, outs[2 * pr], outs[2 * pr + 1]))


def _swa_bias():
    w = SWA_WINDOW
    rows = SWA_GROUP * w
    r = jnp.arange(rows)[:, None] % w
    c = jnp.arange(3 * w)[None, :]
    band = jnp.abs(c - w - r) <= SWA_WINDOW
    visible = jnp.stack([band, band & (c >= w), band & (c < 2 * w)])
    return jnp.where(visible, jnp.inf, MASK_VALUE).astype(jnp.float32)


def _swa(sink, q, k, v, seq):
    t = q.shape[0]
    w = SWA_WINDOW
    per_tile = SWA_TILE // w
    assert per_tile >= 2
    n_blocks = t // w
    kvw = SWA_KV_HEADS * SWA_HEAD_DIM

    prev = pl.BlockSpec((w, kvw), lambda i: (jnp.maximum(i * per_tile - 1, 0), 0))
    cur = pl.BlockSpec((SWA_TILE, kvw), lambda i: (i, 0))
    nxt = pl.BlockSpec((w, kvw), lambda i: (jnp.minimum((i + 1) * per_tile, n_blocks - 1), 0))
    qspec = pl.BlockSpec((SWA_TILE, SWA_HEADS * SWA_HEAD_DIM), lambda i: (i, 0))
    return pl.pallas_call(
        functools.partial(_swa_body, seq=seq),
        grid=(t // SWA_TILE,),
        in_specs=[pl.BlockSpec(memory_space=pltpu.SMEM), _const_spec((3, SWA_GROUP * w, 3 * w)),
                  qspec, prev, cur, nxt, prev, cur, nxt],
        out_specs=qspec,
        out_shape=jax.ShapeDtypeStruct((t, SWA_HEADS * SWA_HEAD_DIM), jnp.bfloat16),
        scratch_shapes=[pltpu.VMEM((SWA_TILE + 2 * w, kvw), jnp.bfloat16),
                        pltpu.VMEM((SWA_TILE + 2 * w, kvw), jnp.bfloat16)],
        compiler_params=_params(("parallel",)),
        name="swa_attn",
    )(sink, _swa_bias(), q, k, k, k, v, v, v)


def _merge_body(x_ref, oa_ref, yb_ref, oc_ref, g_ref, wg_ref, wa_ref, wb_ref, wc_ref, wo_ref, o_ref):
    x = x_ref[...]
    u = _bf16(_rmsnorm(x, g_ref[...]))
    oa = jnp.concatenate([oa_ref[p] for p in range(MLA_HEADS // 2)], axis=1)
    branches = ((oa, wa_ref), (yb_ref[...], wb_ref), (oc_ref[...], wc_ref))
    merged = None
    for b, (y, w_ref) in enumerate(branches):
        gate = _sigmoid(_dot(u, wg_ref[:, b * D_MODEL:(b + 1) * D_MODEL]))
        term = gate * _dot(y, w_ref[...])
        merged = term if merged is None else merged + term
    o_ref[...] = x + _dot(_bf16(merged), wo_ref[...])


def _merge(x, oa, yb, oc, lw, layer):
    t = x.shape[0]

    def tok(width):
        return pl.BlockSpec((TOKEN_TILE, width), lambda i: (i, 0))

    half = MLA_HEADS * MLA_V
    return pl.pallas_call(
        _merge_body,
        grid=(t // TOKEN_TILE,),
        in_specs=[tok(D_MODEL), pl.BlockSpec((MLA_HEADS // 2, TOKEN_TILE, LANES), lambda i: (0, i, 0)),
                  tok(CONV_WIDTH), tok(SWA_HEADS * SWA_HEAD_DIM),
                  _layer_spec((1, D_MODEL), layer), _layer_spec((D_MODEL, 3 * D_MODEL), layer),
                  _layer_spec((half, D_MODEL), layer), _layer_spec((CONV_WIDTH, D_MODEL), layer),
                  _layer_spec((SWA_HEADS * SWA_HEAD_DIM, D_MODEL), layer),
                  _layer_spec((D_MODEL, D_MODEL), layer)],
        out_specs=tok(D_MODEL),
        out_shape=jax.ShapeDtypeStruct((t, D_MODEL), jnp.float32),
        compiler_params=_params(("parallel",)),
        name="merge",
    )(x, oa, yb, oc, lw["mix_norm"], lw["w_gate"], lw["w_a"], lw["w_b"], lw["w_c"], lw["w_o"])


def _rotate_half_cols(w, head_dim):
    lead = w.shape[:-1]
    wh = w.reshape(*lead, -1, 2, head_dim // 2)
    return jnp.concatenate([-wh[..., 1:2, :], wh[..., 0:1, :]], axis=-2).reshape(w.shape)


def _pad_cols(w, width):
    return jnp.pad(w, [(0, 0)] * (w.ndim - 1) + [(0, width - w.shape[-1])])


def _prepare_weights(w_in, mla_w_uq, mla_w_ukv):
    depth = w_in.shape[0]
    o = 0
    q_lat = w_in[..., o:o + MLA_Q_RANK]; o += MLA_Q_RANK
    kv_lat = w_in[..., o:o + MLA_KV_RANK]; o += MLA_KV_RANK
    k_rope = w_in[..., o:o + MLA_ROPE]; o += MLA_ROPE
    conv = w_in[..., o:o + 3 * CONV_WIDTH]; o += 3 * CONV_WIDTH
    sq = w_in[..., o:o + SWA_HEADS * SWA_HEAD_DIM]; o += SWA_HEADS * SWA_HEAD_DIM
    sk = w_in[..., o:o + SWA_KV_HEADS * SWA_HEAD_DIM]; o += SWA_KV_HEADS * SWA_HEAD_DIM
    sv = w_in[..., o:o + SWA_KV_HEADS * SWA_HEAD_DIM]; o += SWA_KV_HEADS * SWA_HEAD_DIM
    w_gate = w_in[..., o:]
    w_main = jnp.concatenate([
        q_lat, kv_lat, conv, sq, _rotate_half_cols(sq, SWA_HEAD_DIM), sk,
        _rotate_half_cols(sk, SWA_HEAD_DIM), sv, _pad_cols(k_rope, LANES),
        _pad_cols(_rotate_half_cols(k_rope, MLA_ROPE), LANES)], axis=-1)

    uq = mla_w_uq.reshape(depth, MLA_Q_RANK, MLA_HEADS, MLA_NOPE + MLA_ROPE)
    nope, rope = uq[..., :MLA_NOPE], uq[..., MLA_NOPE:]
    zeros = jnp.zeros_like
    pad = zeros(rope)
    w_q = jnp.concatenate([
        jnp.concatenate([nope, rope, pad], axis=-1).reshape(depth, MLA_Q_RANK, -1),
        jnp.concatenate([zeros(nope), _rotate_half_cols(rope, MLA_ROPE), pad], axis=-1)
        .reshape(depth, MLA_Q_RANK, -1)], axis=-1)

    ukv = mla_w_ukv.reshape(depth, MLA_KV_RANK, MLA_HEADS, MLA_NOPE + MLA_V)
    k_nope, v = ukv[..., :MLA_NOPE], ukv[..., MLA_NOPE:]
    even = (jnp.arange(MLA_HEADS) % 2 == 0)[:, None]
    v_pad = jnp.where(even, jnp.concatenate([v, zeros(v)], axis=-1), jnp.concatenate([zeros(v), v], axis=-1))
    w_kv = jnp.concatenate([
        jnp.concatenate([k_nope, zeros(k_nope)], axis=-1).reshape(depth, MLA_KV_RANK, -1),
        v_pad.reshape(depth, MLA_KV_RANK, -1)], axis=-1)
    return w_main, w_gate, w_q, w_kv


def _expand_matrix():
    src = jnp.arange(LANES)[:, None]
    dst = jnp.arange(MLA_HEADS * MLA_PAD)[None, :]
    hit = (src < MLA_ROPE) & (dst % MLA_PAD == MLA_NOPE + src)
    return hit.astype(jnp.bfloat16)


def _rope_tables(seq):
    def cos_sin(dim):
        inv = 1.0 / (ROPE_THETA ** (jnp.arange(0, dim, 2, dtype=jnp.float32) / dim))
        ang = jnp.arange(seq, dtype=jnp.float32)[:, None] * inv[None, :]
        return jnp.cos(ang), jnp.sin(ang)

    mc, ms = cos_sin(MLA_ROPE)
    sc, ss = cos_sin(SWA_HEAD_DIM)
    scale = (MLA_NOPE + MLA_ROPE) ** -0.5 * LOG2_E
    ones = jnp.ones((seq, MLA_NOPE), jnp.float32)
    zq = jnp.zeros((seq, MLA_PAD - MLA_NOPE - MLA_ROPE), jnp.float32)
    zn = jnp.zeros((seq, MLA_NOPE), jnp.float32)
    zk = jnp.zeros((seq, LANES - MLA_ROPE), jnp.float32)
    cols = [
        scale * jnp.concatenate([ones, mc, mc, zq], axis=1),
        scale * jnp.concatenate([zn, ms, ms, zq], axis=1),
        jnp.concatenate([mc, mc, zk], axis=1),
        jnp.concatenate([ms, ms, zk], axis=1),
        jnp.concatenate([sc, sc, sc, sc], axis=1),
        jnp.concatenate([ss, ss, ss, ss], axis=1),
    ]
    return jnp.concatenate(cols, axis=1)


def _encode(x, lw, depth, final_norm):
    batch, seq, _ = x.shape
    tables = _rope_tables(seq)
    h = x.reshape(batch * seq, D_MODEL)
    for layer in range(depth):
        h = _ffn(h, lw["ffn1_norm"], lw["ffn1_w_gu"], lw["ffn1_w_down"], final_norm, layer, False)
        q, k, v, yb, sq, sk, sv = _proj(h, tables, lw, layer, seq)
        oa = _mla(q, k, v, batch, seq)
        oc = _swa(lw["sink"][layer], sq, sk, sv, seq)
        h = _merge(h, oa, yb, oc, lw, layer)
        h = _ffn(h, lw["ffn2_norm"], lw["ffn2_w_gu"], lw["ffn2_w_down"], final_norm, layer,
                 layer == depth - 1)
    return h.reshape(batch, seq, D_MODEL)


def kernel(x_prompt, x_sample, ffn1_norm, ffn1_w_gu, ffn1_w_down, mix_norm, w_in, mla_q_norm,
           mla_w_uq, mla_kv_norm, mla_w_ukv, mla_w_o, conv_w, conv_w_o, swa_sink, swa_w_o, w_o,
           ffn2_norm, ffn2_w_gu, ffn2_w_down, final_norm):
    depth = w_in.shape[0]
    w_main, w_gate, w_q, w_kv = _prepare_weights(_bf16(w_in), _bf16(mla_w_uq), _bf16(mla_w_ukv))
    lw = {
        "ffn1_norm": ffn1_norm[:, None, :], "ffn1_w_gu": _bf16(ffn1_w_gu), "ffn1_w_down": _bf16(ffn1_w_down),
        "ffn2_norm": ffn2_norm[:, None, :], "ffn2_w_gu": _bf16(ffn2_w_gu), "ffn2_w_down": _bf16(ffn2_w_down),
        "mix_norm": mix_norm[:, None, :], "w_main": w_main, "w_gate": w_gate,
        "q_norm": mla_q_norm[:, None, :], "w_q": w_q, "kv_norm": mla_kv_norm[:, None, :], "w_kv": w_kv,
        "expand": _expand_matrix(),
        "conv_w": jnp.pad(conv_w, ((0, 0), (0, HALO - conv_w.shape[1]), (0, 0))), "sink": swa_sink,
        "w_a": _bf16(mla_w_o), "w_b": _bf16(conv_w_o), "w_c": _bf16(swa_w_o), "w_o": _bf16(w_o),
    }
    fin = final_norm[None, :]
    return (_encode(x_prompt, lw, depth, fin), _encode(x_sample, lw, depth, fin))
```

```python
import functools

import jax
import jax.numpy as jnp
from jax import lax
from jax.experimental import pallas as pl
from jax.experimental.pallas import tpu as pltpu

D_MODEL = 1024
D_FF = 2816
NORM_EPS = 1e-6
ROPE_THETA = 10000.0

MLA_HEADS = 8
MLA_Q_RANK = 256
MLA_KV_RANK = 128
MLA_NOPE = 64
MLA_ROPE = 32
MLA_V = 64

CONV_WIDTH = 512

SWA_HEADS = 8
SWA_KV_HEADS = 2
SWA_HEAD_DIM = 64
SWA_WINDOW = 128
SWA_GROUP = SWA_HEADS // SWA_KV_HEADS

LANES = 128
HALF = LANES // 2
MLA_PAD = LANES
MASK_VALUE = -1e30
LOG2_E = 1.4426950408889634

TOKEN_TILE = 512
FF_CHUNK = 1408
MLA_TQ = 256
MLA_S_BYTES = 16 * 1024 * 1024
MLA_TK = 2048
MLA_QBLOCK = 2048
MLA_GROUP = 4
SWA_TILE = 512
HALO = 8
VMEM_LIMIT = 56 * 1024 * 1024

C_QLAT = 0
C_KVLAT = C_QLAT + MLA_Q_RANK
C_CONV = C_KVLAT + MLA_KV_RANK
C_SQ = C_CONV + 3 * CONV_WIDTH
C_SQR = C_SQ + SWA_HEADS * SWA_HEAD_DIM
C_SK = C_SQR + SWA_HEADS * SWA_HEAD_DIM
C_SKR = C_SK + SWA_KV_HEADS * SWA_HEAD_DIM
C_SV = C_SKR + SWA_KV_HEADS * SWA_HEAD_DIM
C_KR = C_SV + SWA_KV_HEADS * SWA_HEAD_DIM
C_KRR = C_KR + LANES
C_MAIN = C_KRR + LANES

T_CQ, T_SQ, T_CK, T_SK, T_CS, T_SS = range(6)


def _bf16(x):
    return x.astype(jnp.bfloat16)


def _dot(a, b):
    return jnp.dot(a, b, preferred_element_type=jnp.float32)


def _dot_nt(a, b):
    return lax.dot_general(a, b, (((1,), (1,)), ((), ())), preferred_element_type=jnp.float32)


def _rmsnorm(x, g):
    return x * lax.rsqrt(jnp.mean(x * x, axis=-1, keepdims=True) + NORM_EPS) * g


def _sigmoid(x):
    return 1.0 / (1.0 + jnp.exp(-x))


def _const_spec(shape):
    return pl.BlockSpec(shape, lambda *_: (0,) * len(shape), pipeline_mode=pl.Buffered(1))


def _layer_spec(shape, layer):
    return pl.BlockSpec((None,) + shape, lambda *_: (layer,) + (0,) * len(shape),
                        pipeline_mode=pl.Buffered(1))


def _params(sem):
    return pltpu.CompilerParams(dimension_semantics=sem, vmem_limit_bytes=VMEM_LIMIT)


def _ffn_body(x_ref, g_ref, wgu_ref, wd_ref, fin_ref, o_ref, h_ref, *, final):
    x = x_ref[...]
    xn = _bf16(_rmsnorm(x, g_ref[...]))
    for c in range(D_FF // FF_CHUNK):
        a = _dot(xn, wgu_ref[:, c * FF_CHUNK:(c + 1) * FF_CHUNK])
        b = _dot(xn, wgu_ref[:, D_FF + c * FF_CHUNK:D_FF + (c + 1) * FF_CHUNK])
        h_ref[:, c * FF_CHUNK:(c + 1) * FF_CHUNK] = _bf16(a * _sigmoid(a) * b)
    y = x + 0.5 * _dot(h_ref[...], wd_ref[...])
    if final:
        y = _rmsnorm(y, fin_ref[...])
    o_ref[...] = y


def _ffn(x, g, wgu, wd, fin, layer, final):
    t = x.shape[0]
    tok = pl.BlockSpec((TOKEN_TILE, D_MODEL), lambda i: (i, 0))
    return pl.pallas_call(
        functools.partial(_ffn_body, final=final),
        grid=(t // TOKEN_TILE,),
        in_specs=[tok, _layer_spec((1, D_MODEL), layer), _layer_spec((D_MODEL, 2 * D_FF), layer),
                  _layer_spec((D_FF, D_MODEL), layer), _const_spec((1, D_MODEL))],
        out_specs=tok,
        out_shape=jax.ShapeDtypeStruct((t, D_MODEL), jnp.float32),
        scratch_shapes=[pltpu.VMEM((TOKEN_TILE, D_FF), jnp.bfloat16)],
        compiler_params=_params(("parallel",)),
        name="ffn_final" if final else "ffn",
    )(x, g, wgu, wd, fin)


def _proj_body(x_ref, xp_ref, xn_ref, tab_ref, g_ref, wm_ref, qn_ref, wq_ref, kvn_ref, wkv_ref,
               cw_ref, q_ref, k_ref, v_ref, yb_ref, sq_ref, sk_ref, sv_ref, *, seq):
    i = pl.program_id(0)
    g = g_ref[...]
    u = _bf16(_rmsnorm(x_ref[...], g))

    def tab(j):
        return tab_ref[:, j * LANES:(j + 1) * LANES]

    def proj(c0, c1):
        return _dot(u, wm_ref[:, c0:c1])

    rq = _bf16(_rmsnorm(proj(C_QLAT, C_KVLAT), qn_ref[...]))
    qq = _dot(rq, wq_ref[...])
    for h in range(MLA_HEADS):
        blk = qq[:, h * MLA_PAD:(h + 1) * MLA_PAD]
        rot = qq[:, (MLA_HEADS + h) * MLA_PAD:(MLA_HEADS + h + 1) * MLA_PAD]
        q_ref[h] = _bf16(blk * tab(T_CQ) + rot * tab(T_SQ))

    rkv = _bf16(_rmsnorm(proj(C_KVLAT, C_CONV), kvn_ref[...]))
    kvv = _dot(rkv, wkv_ref[...])
    kr = proj(C_KR, C_KRR) * tab(T_CK) + proj(C_KRR, C_MAIN) * tab(T_SK)
    kr = pltpu.roll(kr, MLA_NOPE, axis=1)
    lane = lax.broadcasted_iota(jnp.int32, (1, LANES), 1)
    for h in range(MLA_HEADS):
        sl = slice(h * MLA_PAD, (h + 1) * MLA_PAD)
        k_ref[h] = _bf16(kvv[:, sl] + kr)
        ones = jnp.where((lane >= HALF) == (h % 2 == 0), 1.0, 0.0)
        v_ref[h] = _bf16(kvv[:, MLA_HEADS * MLA_PAD + h * LANES:MLA_HEADS * MLA_PAD + (h + 1) * LANES] + ones)

    b_gate = proj(C_CONV, C_CONV + CONV_WIDTH)
    xh = jnp.concatenate([xp_ref[...], xn_ref[...]], axis=0)
    uh = _bf16(_rmsnorm(xh, g))
    cx = _dot(jnp.concatenate([u, uh], axis=0), wm_ref[:, C_CONV + CONV_WIDTH:C_SQ])
    zz = cx[:, :CONV_WIDTH] * cx[:, CONV_WIDTH:]
    z = zz[:TOKEN_TILE]
    zh = zz[TOKEN_TILE:]
    row0 = i * TOKEN_TILE
    has_prev = (row0 % seq) != 0
    has_next = ((row0 + TOKEN_TILE) % seq) != 0
    z_prev = jnp.where(has_prev, zh[HALO - 1:HALO, :], 0.0)
    z_next = jnp.where(has_next, zh[HALO:HALO + 1, :], 0.0)
    rows = lax.broadcasted_iota(jnp.int32, (TOKEN_TILE, CONV_WIDTH), 0)
    z_up = jnp.where(rows == 0, z_prev, pltpu.roll(z, 1, axis=0))
    z_dn = jnp.where(rows == TOKEN_TILE - 1, z_next, pltpu.roll(z, TOKEN_TILE - 1, axis=0))
    y = z_up * cw_ref[0:1, :] + z * cw_ref[1:2, :] + z_dn * cw_ref[2:3, :]
    yb_ref[...] = _bf16(b_gate * y)

    sq = proj(C_SQ, C_SQR)
    sqr = proj(C_SQR, C_SK)
    scale = SWA_HEAD_DIM ** -0.5 * LOG2_E
    for p in range(SWA_HEADS * SWA_HEAD_DIM // LANES):
        sl = slice(p * LANES, (p + 1) * LANES)
        sq_ref[:, sl] = _bf16((sq[:, sl] * tab(T_CS) + sqr[:, sl] * tab(T_SS)) * scale)
    sk_ref[...] = _bf16(proj(C_SK, C_SKR) * tab(T_CS) + proj(C_SKR, C_SV) * tab(T_SS))
    sv_ref[...] = _bf16(proj(C_SV, C_KR))


def _proj(x, tables, lw, layer, seq):
    t = x.shape[0]
    nt = t // TOKEN_TILE
    tiles_per_seq = seq // TOKEN_TILE
    halo_per_tile = TOKEN_TILE // HALO
    n_halo = t // HALO

    def tok(width):
        return pl.BlockSpec((TOKEN_TILE, width), lambda i: (i, 0))

    def out(width):
        return jax.ShapeDtypeStruct((t, width), jnp.bfloat16)

    in_specs = [
        tok(D_MODEL),
        pl.BlockSpec((HALO, D_MODEL), lambda i: (jnp.maximum(i * halo_per_tile - 1, 0), 0)),
        pl.BlockSpec((HALO, D_MODEL), lambda i: (jnp.minimum((i + 1) * halo_per_tile, n_halo - 1), 0)),
        pl.BlockSpec((TOKEN_TILE, 6 * LANES), lambda i: (i % tiles_per_seq, 0)),
        _layer_spec((1, D_MODEL), layer),
        _layer_spec((D_MODEL, C_MAIN), layer),
        _layer_spec((1, MLA_Q_RANK), layer),
        _layer_spec((MLA_Q_RANK, 2 * MLA_HEADS * MLA_PAD), layer),
        _layer_spec((1, MLA_KV_RANK), layer),
        _layer_spec((MLA_KV_RANK, MLA_HEADS * (MLA_PAD + LANES)), layer),
        _layer_spec((HALO, CONV_WIDTH), layer),
    ]
    widths = [CONV_WIDTH, SWA_HEADS * SWA_HEAD_DIM, SWA_KV_HEADS * SWA_HEAD_DIM,
              SWA_KV_HEADS * SWA_HEAD_DIM]
    head_spec = pl.BlockSpec((MLA_HEADS, TOKEN_TILE, LANES), lambda i: (0, i, 0))
    head_shape = jax.ShapeDtypeStruct((MLA_HEADS, t, LANES), jnp.bfloat16)
    return pl.pallas_call(
        functools.partial(_proj_body, seq=seq),
        grid=(nt,),
        in_specs=in_specs,
        out_specs=[head_spec] * 3 + [tok(w) for w in widths],
        out_shape=[head_shape] * 3 + [out(w) for w in widths],
        compiler_params=_params(("parallel",)),
        name="proj",
    )(x, x, x, tables, lw["mix_norm"], lw["w_main"], lw["q_norm"], lw["w_q"], lw["kv_norm"],
      lw["w_kv"], lw["conv_w"])


def _mla_body(q_ref, k_ref, v_ref, o_ref, s_ref, m_ref, o_scr, *, seq, sub):
    n_chunks = seq // MLA_TK
    item_rows = sub * MLA_TQ
    n_items = (MLA_QBLOCK // item_rows) * MLA_GROUP
    low = lax.broadcasted_iota(jnp.int32, (MLA_TQ, LANES), 1) < HALF

    def rows_of(t, u):
        return pl.ds(pl.multiple_of((t // MLA_GROUP) * item_rows + u * MLA_TQ, MLA_TQ), MLA_TQ)

    def pass_a(t):
        head = t % MLA_GROUP
        for u in range(sub):
            q = q_ref[head, rows_of(t, u), :]
            mm = None
            for c in range(n_chunks):
                s = _dot_nt(q, k_ref[head, c * MLA_TK:(c + 1) * MLA_TK, :])
                s_ref[t % 2, u, c] = s
                cm = jnp.max(s, axis=-1, keepdims=True)
                mm = cm if mm is None else jnp.maximum(mm, cm)
            m_ref[t % 2, u] = jnp.broadcast_to(mm, (MLA_TQ, LANES))

    def pass_b(t):
        head = t % MLA_GROUP
        odd = (head % 2) == 1
        valid = jnp.logical_xor(low, odd)
        for u in range(sub):
            m = m_ref[t % 2, u][:, :1]
            res = None
            for c in range(n_chunks):
                p = _bf16(jnp.exp2(s_ref[t % 2, u, c] - m))
                pv = _dot(p, v_ref[head, c * MLA_TK:(c + 1) * MLA_TK, :])
                res = pv if res is None else res + pv
            val = res / pltpu.roll(res, HALF, axis=1)
            new = jnp.where(valid, val, jnp.where(odd, o_scr[u], 0.0))
            o_scr[u] = new
            o_ref[head // 2, rows_of(t, u), :] = _bf16(new)

    o_scr[...] = jnp.zeros((sub, MLA_TQ, LANES), jnp.float32)
    pass_a(0)

    def item(t, carry):
        pass_a(t)
        pass_b(t - 1)
        return carry

    lax.fori_loop(1, n_items, item, 0)
    pass_b(n_items - 1)


def _mla(q, k, v, batch, seq):
    t = q.shape[1]
    sub = max(1, min(MLA_QBLOCK // MLA_TQ, MLA_S_BYTES // (2 * 4 * seq * MLA_TQ)))
    nqb = seq // MLA_QBLOCK
    groups = MLA_HEADS // MLA_GROUP
    kv_spec = pl.BlockSpec((MLA_GROUP, seq, LANES), lambda b, g, i: (g, b, 0), pipeline_mode=pl.Buffered(1))
    return pl.pallas_call(
        functools.partial(_mla_body, seq=seq, sub=sub),
        grid=(batch, groups, nqb),
        in_specs=[pl.BlockSpec((MLA_GROUP, MLA_QBLOCK, LANES), lambda b, g, i: (g, b * nqb + i, 0)),
                  kv_spec, kv_spec],
        out_specs=pl.BlockSpec((MLA_GROUP // 2, MLA_QBLOCK, LANES), lambda b, g, i: (g, b * nqb + i, 0)),
        out_shape=jax.ShapeDtypeStruct((MLA_HEADS // 2, t, LANES), jnp.bfloat16),
        scratch_shapes=[
            pltpu.VMEM((2, sub, seq // MLA_TK, MLA_TQ, MLA_TK), jnp.float32),
            pltpu.VMEM((2, sub, MLA_TQ, LANES), jnp.float32),
            pltpu.VMEM((sub, MLA_TQ, LANES), jnp.float32),
        ],
        compiler_params=_params(("parallel", "parallel", "arbitrary")),
        name="mla_attn",
    )(q, k, v)


def _swa_body(sink_ref, bias_ref, q_ref, kp_ref, kc_ref, kn_ref, vp_ref, vc_ref, vn_ref, o_ref,
              kbuf, vbuf, *, seq):
    w = SWA_WINDOW
    n_sub = SWA_TILE // w
    i = pl.program_id(0)
    kbuf[0:w, :] = kp_ref[...]
    kbuf[w:w + SWA_TILE, :] = kc_ref[...]
    kbuf[w + SWA_TILE:, :] = kn_ref[...]
    vbuf[0:w, :] = vp_ref[...]
    vbuf[w:w + SWA_TILE, :] = vc_ref[...]
    vbuf[w + SWA_TILE:, :] = vn_ref[...]

    rows = SWA_GROUP * w
    head_row = lax.broadcasted_iota(jnp.int32, (rows, 1), 0) // w
    lane_q = lax.broadcasted_iota(jnp.int32, (w, LANES), 1)
    pos0 = (i * SWA_TILE) % seq
    first = jnp.where(pos0 == 0, 1, 0)
    last = jnp.where(pos0 + SWA_TILE == seq, 2, 0)

    for n in range(n_sub):
        bias = bias_ref[first if n == 0 else (last if n == n_sub - 1 else 0)]
        kb = kbuf[n * w:(n + 3) * w, :]
        vb = vbuf[n * w:(n + 3) * w, :]
        outs = [None] * SWA_HEADS
        for g in range(SWA_KV_HEADS):
            keep = (lane_q >= HALF) if g else (lane_q < HALF)
            parts = []
            sink = jnp.zeros((rows, 1), jnp.float32)
            for a in range(SWA_GROUP):
                head = g * SWA_GROUP + a
                blk = q_ref[n * w:(n + 1) * w, (head // 2) * LANES:(head // 2 + 1) * LANES]
                blk = blk.astype(jnp.float32)
                if head % 2 != g:
                    blk = pltpu.roll(blk, HALF, axis=1)
                parts.append(_bf16(jnp.where(keep, blk, 0.0)))
                sink = jnp.where(head_row == a, sink_ref[head] * LOG2_E, sink)
            qs = jnp.concatenate(parts, axis=0)
            s = jnp.minimum(_dot_nt(qs, kb), bias)
            m = jnp.maximum(jnp.max(s, axis=-1, keepdims=True), sink)
            p = jnp.exp2(s - m)
            denom = jnp.sum(p, axis=-1, keepdims=True) + jnp.exp2(sink - m)
            pv = _dot(_bf16(p), vb) * (1.0 / denom)
            for a in range(SWA_GROUP):
                head = g * SWA_GROUP + a
                blk = pv[a * w:(a + 1) * w, :]
                if head % 2 != g:
                    blk = pltpu.roll(blk, HALF, axis=1)
                outs[head] = blk
        for pr in range(SWA_HEADS // 2):
            o_ref[n * w:(n + 1) * w, pr * LANES:(pr + 1) * LANES] = _bf16(
                jnp.where(lane_q < HALF, outs[2 * pr], outs[2 * pr + 1]))


def _swa_bias():
    w = SWA_WINDOW
    rows = SWA_GROUP * w
    r = jnp.arange(rows)[:, None] % w
    c = jnp.arange(3 * w)[None, :]
    band = jnp.abs(c - w - r) <= SWA_WINDOW
    visible = jnp.stack([band, band & (c >= w), band & (c < 2 * w)])
    return jnp.where(visible, jnp.inf, MASK_VALUE).astype(jnp.float32)


def _swa(sink, q, k, v, seq):
    t = q.shape[0]
    w = SWA_WINDOW
    per_tile = SWA_TILE // w
    assert per_tile >= 2
    n_blocks = t // w
    kvw = SWA_KV_HEADS * SWA_HEAD_DIM

    prev = pl.BlockSpec((w, kvw), lambda i: (jnp.maximum(i * per_tile - 1, 0), 0))
    cur = pl.BlockSpec((SWA_TILE, kvw), lambda i: (i, 0))
    nxt = pl.BlockSpec((w, kvw), lambda i: (jnp.minimum((i + 1) * per_tile, n_blocks - 1), 0))
    qspec = pl.BlockSpec((SWA_TILE, SWA_HEADS * SWA_HEAD_DIM), lambda i: (i, 0))
    return pl.pallas_call(
        functools.partial(_swa_body, seq=seq),
        grid=(t // SWA_TILE,),
        in_specs=[pl.BlockSpec(memory_space=pltpu.SMEM), _const_spec((3, SWA_GROUP * w, 3 * w)),
                  qspec, prev, cur, nxt, prev, cur, nxt],
        out_specs=qspec,
        out_shape=jax.ShapeDtypeStruct((t, SWA_HEADS * SWA_HEAD_DIM), jnp.bfloat16),
        scratch_shapes=[pltpu.VMEM((SWA_TILE + 2 * w, kvw), jnp.bfloat16),
                        pltpu.VMEM((SWA_TILE + 2 * w, kvw), jnp.bfloat16)],
        compiler_params=_params(("parallel",)),
        name="swa_attn",
    )(sink, _swa_bias(), q, k, k, k, v, v, v)


def _merge_body(x_ref, oa_ref, yb_ref, oc_ref, g_ref, wg_ref, wa_ref, wb_ref, wc_ref, wo_ref, o_ref):
    x = x_ref[...]
    u = _bf16(_rmsnorm(x, g_ref[...]))
    oa = jnp.concatenate([oa_ref[p] for p in range(MLA_HEADS // 2)], axis=1)
    branches = ((oa, wa_ref), (yb_ref[...], wb_ref), (oc_ref[...], wc_ref))
    merged = None
    for b, (y, w_ref) in enumerate(branches):
        gate = _sigmoid(_dot(u, wg_ref[:, b * D_MODEL:(b + 1) * D_MODEL]))
        term = gate * _dot(y, w_ref[...])
        merged = term if merged is None else merged + term
    o_ref[...] = x + _dot(_bf16(merged), wo_ref[...])


def _merge(x, oa, yb, oc, lw, layer):
    t = x.shape[0]

    def tok(width):
        return pl.BlockSpec((TOKEN_TILE, width), lambda i: (i, 0))

    half = MLA_HEADS * MLA_V
    return pl.pallas_call(
        _merge_body,
        grid=(t // TOKEN_TILE,),
        in_specs=[tok(D_MODEL), pl.BlockSpec((MLA_HEADS // 2, TOKEN_TILE, LANES), lambda i: (0, i, 0)),
                  tok(CONV_WIDTH), tok(SWA_HEADS * SWA_HEAD_DIM),
                  _layer_spec((1, D_MODEL), layer), _layer_spec((D_MODEL, 3 * D_MODEL), layer),
                  _layer_spec((half, D_MODEL), layer), _layer_spec((CONV_WIDTH, D_MODEL), layer),
                  _layer_spec((SWA_HEADS * SWA_HEAD_DIM, D_MODEL), layer),
                  _layer_spec((D_MODEL, D_MODEL), layer)],
        out_specs=tok(D_MODEL),
        out_shape=jax.ShapeDtypeStruct((t, D_MODEL), jnp.float32),
        compiler_params=_params(("parallel",)),
        name="merge",
    )(x, oa, yb, oc, lw["mix_norm"], lw["w_gate"], lw["w_a"], lw["w_b"], lw["w_c"], lw["w_o"])


def _rotate_half_cols(w, head_dim):
    lead = w.shape[:-1]
    wh = w.reshape(*lead, -1, 2, head_dim // 2)
    return jnp.concatenate([-wh[..., 1:2, :], wh[..., 0:1, :]], axis=-2).reshape(w.shape)


def _pad_cols(w, width):
    return jnp.pad(w, [(0, 0)] * (w.ndim - 1) + [(0, width - w.shape[-1])])


def _prepare_weights(w_in, mla_w_uq, mla_w_ukv):
    depth = w_in.shape[0]
    o = 0
    q_lat = w_in[..., o:o + MLA_Q_RANK]; o += MLA_Q_RANK
    kv_lat = w_in[..., o:o + MLA_KV_RANK]; o += MLA_KV_RANK
    k_rope = w_in[..., o:o + MLA_ROPE]; o += MLA_ROPE
    conv = w_in[..., o:o + 3 * CONV_WIDTH]; o += 3 * CONV_WIDTH
    sq = w_in[..., o:o + SWA_HEADS * SWA_HEAD_DIM]; o += SWA_HEADS * SWA_HEAD_DIM
    sk = w_in[..., o:o + SWA_KV_HEADS * SWA_HEAD_DIM]; o += SWA_KV_HEADS * SWA_HEAD_DIM
    sv = w_in[..., o:o + SWA_KV_HEADS * SWA_HEAD_DIM]; o += SWA_KV_HEADS * SWA_HEAD_DIM
    w_gate = w_in[..., o:]
    w_main = jnp.concatenate([
        q_lat, kv_lat, conv, sq, _rotate_half_cols(sq, SWA_HEAD_DIM), sk,
        _rotate_half_cols(sk, SWA_HEAD_DIM), sv, _pad_cols(k_rope, LANES),
        _pad_cols(_rotate_half_cols(k_rope, MLA_ROPE), LANES)], axis=-1)

    uq = mla_w_uq.reshape(depth, MLA_Q_RANK, MLA_HEADS, MLA_NOPE + MLA_ROPE)
    nope, rope = uq[..., :MLA_NOPE], uq[..., MLA_NOPE:]
    zeros = jnp.zeros_like
    pad = zeros(rope)
    w_q = jnp.concatenate([
        jnp.concatenate([nope, rope, pad], axis=-1).reshape(depth, MLA_Q_RANK, -1),
        jnp.concatenate([zeros(nope), _rotate_half_cols(rope, MLA_ROPE), pad], axis=-1)
        .reshape(depth, MLA_Q_RANK, -1)], axis=-1)

    ukv = mla_w_ukv.reshape(depth, MLA_KV_RANK, MLA_HEADS, MLA_NOPE + MLA_V)
    k_nope, v = ukv[..., :MLA_NOPE], ukv[..., MLA_NOPE:]
    even = (jnp.arange(MLA_HEADS) % 2 == 0)[:, None]
    v_pad = jnp.where(even, jnp.concatenate([v, zeros(v)], axis=-1), jnp.concatenate([zeros(v), v], axis=-1))
    w_kv = jnp.concatenate([
        jnp.concatenate([k_nope, zeros(k_nope)], axis=-1).reshape(depth, MLA_KV_RANK, -1),
        v_pad.reshape(depth, MLA_KV_RANK, -1)], axis=-1)
    return w_main, w_gate, w_q, w_kv


def _rope_tables(seq):
    def cos_sin(dim):
        inv = 1.0 / (ROPE_THETA ** (jnp.arange(0, dim, 2, dtype=jnp.float32) / dim))
        ang = jnp.arange(seq, dtype=jnp.float32)[:, None] * inv[None, :]
        return jnp.cos(ang), jnp.sin(ang)

    mc, ms = cos_sin(MLA_ROPE)
    sc, ss = cos_sin(SWA_HEAD_DIM)
    scale = (MLA_NOPE + MLA_ROPE) ** -0.5 * LOG2_E
    ones = jnp.ones((seq, MLA_NOPE), jnp.float32)
    zq = jnp.zeros((seq, MLA_PAD - MLA_NOPE - MLA_ROPE), jnp.float32)
    zn = jnp.zeros((seq, MLA_NOPE), jnp.float32)
    zk = jnp.zeros((seq, LANES - MLA_ROPE), jnp.float32)
    cols = [
        scale * jnp.concatenate([ones, mc, mc, zq], axis=1),
        scale * jnp.concatenate([zn, ms, ms, zq], axis=1),
        jnp.concatenate([mc, mc, zk], axis=1),
        jnp.concatenate([ms, ms, zk], axis=1),
        jnp.concatenate([sc, sc, sc, sc], axis=1),
        jnp.concatenate([ss, ss, ss, ss], axis=1),
    ]
    return jnp.concatenate(cols, axis=1)


def _encode(x, lw, depth, final_norm):
    batch, seq, _ = x.shape
    tables = _rope_tables(seq)
    h = x.reshape(batch * seq, D_MODEL)
    for layer in range(depth):
        h = _ffn(h, lw["ffn1_norm"], lw["ffn1_w_gu"], lw["ffn1_w_down"], final_norm, layer, False)
        q, k, v, yb, sq, sk, sv = _proj(h, tables, lw, layer, seq)
        oa = _mla(q, k, v, batch, seq)
        oc = _swa(lw["sink"][layer], sq, sk, sv, seq)
        h = _merge(h, oa, yb, oc, lw, layer)
        h = _ffn(h, lw["ffn2_norm"], lw["ffn2_w_gu"], lw["ffn2_w_down"], final_norm, layer,
                 layer == depth - 1)
    return h.reshape(batch, seq, D_MODEL)


def kernel(x_prompt, x_sample, ffn1_norm, ffn1_w_gu, ffn1_w_down, mix_norm, w_in, mla_q_norm,
           mla_w_uq, mla_kv_norm, mla_w_ukv, mla_w_o, conv_w, conv_w_o, swa_sink, swa_w_o, w_o,
           ffn2_norm, ffn2_w_gu, ffn2_w_down, final_norm):
    depth = w_in.shape[0]
    w_main, w_gate, w_q, w_kv = _prepare_weights(_bf16(w_in), _bf16(mla_w_uq), _bf16(mla_w_ukv))
    lw = {
        "ffn1_norm": ffn1_norm[:, None, :], "ffn1_w_gu": _bf16(ffn1_w_gu), "ffn1_w_down": _bf16(ffn1_w_down),
        "ffn2_norm": ffn2_norm[:, None, :], "ffn2_w_gu": _bf16(ffn2_w_gu), "ffn2_w_down": _bf16(ffn2_w_down),
        "mix_norm": mix_norm[:, None, :], "w_main": w_main, "w_gate": w_gate,
        "q_norm": mla_q_norm[:, None, :], "w_q": w_q, "kv_norm": mla_kv_norm[:, None, :], "w_kv": w_kv,
        "conv_w": jnp.pad(conv_w, ((0, 0), (0, HALO - conv_w.shape[1]), (0, 0))), "sink": swa_sink,
        "w_a": _bf16(mla_w_o), "w_b": _bf16(conv_w_o), "w_c": _bf16(swa_w_o), "w_o": _bf16(w_o),
    }
    fin = final_norm[None, :]
    return (_encode(x_prompt, lw, depth, fin), _encode(x_sample, lw, depth, fin))
```

```python
import functools

import jax
import jax.numpy as jnp
from jax import lax
from jax.experimental import pallas as pl
from jax.experimental.pallas import tpu as pltpu

D_MODEL = 1024
D_FF = 2816
NORM_EPS = 1e-6
ROPE_THETA = 10000.0

MLA_HEADS = 8
MLA_Q_RANK = 256
MLA_KV_RANK = 128
MLA_NOPE = 64
MLA_ROPE = 32
MLA_V = 64

CONV_WIDTH = 512

SWA_HEADS = 8
SWA_KV_HEADS = 2
SWA_HEAD_DIM = 64
SWA_WINDOW = 128
SWA_GROUP = SWA_HEADS // SWA_KV_HEADS

LANES = 128
HALF = LANES // 2
MLA_PAD = LANES
MASK_VALUE = -1e30
LOG2_E = 1.4426950408889634

TOKEN_TILE = 512
FF_CHUNK = 1408
MLA_TQ = 256
MLA_S_BYTES = 16 * 1024 * 1024
MLA_TK = 2048
MLA_QBLOCK = 2048
MLA_GROUP = 4
SWA_TILE = 512
SUBLANES = 8
HALO = SUBLANES
VMEM_LIMIT = 56 * 1024 * 1024

C_QLAT = 0
C_KVLAT = C_QLAT + MLA_Q_RANK
C_CONV = C_KVLAT + MLA_KV_RANK
C_SQ = C_CONV + 3 * CONV_WIDTH
C_SQR = C_SQ + SWA_HEADS * SWA_HEAD_DIM
C_SK = C_SQR + SWA_HEADS * SWA_HEAD_DIM
C_SKR = C_SK + SWA_KV_HEADS * SWA_HEAD_DIM
C_SV = C_SKR + SWA_KV_HEADS * SWA_HEAD_DIM
C_KR = C_SV + SWA_KV_HEADS * SWA_HEAD_DIM
C_KRR = C_KR + LANES
C_MAIN = C_KRR + LANES

T_CQ, T_SQ, T_CK, T_SK, T_CS, T_SS = range(6)


def _bf16(x):
    return x.astype(jnp.bfloat16)


def _dot(a, b):
    return jnp.dot(a, b, preferred_element_type=jnp.float32)


def _dot_nt(a, b):
    return lax.dot_general(a, b, (((1,), (1,)), ((), ())), preferred_element_type=jnp.float32)


def _rmsnorm(x, g):
    return x * lax.rsqrt(jnp.mean(x * x, axis=-1, keepdims=True) + NORM_EPS) * g


def _sigmoid(x):
    return 1.0 / (1.0 + jnp.exp(-x))


def _const_spec(shape):
    return pl.BlockSpec(shape, lambda *_: (0,) * len(shape), pipeline_mode=pl.Buffered(1))


def _layer_spec(shape, layer):
    return pl.BlockSpec((None,) + shape, lambda *_: (layer,) + (0,) * len(shape),
                        pipeline_mode=pl.Buffered(1))


def _params(sem):
    return pltpu.CompilerParams(dimension_semantics=sem, vmem_limit_bytes=VMEM_LIMIT)


def _ffn_body(x_ref, g_ref, wgu_ref, wd_ref, fin_ref, o_ref, h_ref, *, final):
    x = x_ref[...]
    xn = _bf16(_rmsnorm(x, g_ref[...]))
    for c in range(D_FF // FF_CHUNK):
        a = _dot(xn, wgu_ref[:, c * FF_CHUNK:(c + 1) * FF_CHUNK])
        b = _dot(xn, wgu_ref[:, D_FF + c * FF_CHUNK:D_FF + (c + 1) * FF_CHUNK])
        h_ref[:, c * FF_CHUNK:(c + 1) * FF_CHUNK] = _bf16(a * _sigmoid(a) * b)
    y = x + 0.5 * _dot(h_ref[...], wd_ref[...])
    if final:
        y = _rmsnorm(y, fin_ref[...])
    o_ref[...] = y


def _ffn(x, g, wgu, wd, fin, layer, final):
    t = x.shape[0]
    tok = pl.BlockSpec((TOKEN_TILE, D_MODEL), lambda i: (i, 0))
    return pl.pallas_call(
        functools.partial(_ffn_body, final=final),
        grid=(t // TOKEN_TILE,),
        in_specs=[tok, _layer_spec((1, D_MODEL), layer), _layer_spec((D_MODEL, 2 * D_FF), layer),
                  _layer_spec((D_FF, D_MODEL), layer), _const_spec((1, D_MODEL))],
        out_specs=tok,
        out_shape=jax.ShapeDtypeStruct((t, D_MODEL), jnp.float32),
        scratch_shapes=[pltpu.VMEM((TOKEN_TILE, D_FF), jnp.bfloat16)],
        compiler_params=_params(("parallel",)),
        name="ffn_final" if final else "ffn",
    )(x, g, wgu, wd, fin)


def _proj_body(x_ref, xp_ref, xn_ref, tab_ref, g_ref, wm_ref, qn_ref, wq_ref, kvn_ref, wkv_ref,
               cw_ref, q_ref, k_ref, v_ref, yb_ref, sq_ref, sk_ref, sv_ref, *, seq):
    i = pl.program_id(0)
    g = g_ref[...]
    u = _bf16(_rmsnorm(x_ref[...], g))

    def tab(j):
        return tab_ref[:, j * LANES:(j + 1) * LANES]

    def proj(c0, c1):
        return _dot(u, wm_ref[:, c0:c1])

    rq = _bf16(_rmsnorm(proj(C_QLAT, C_KVLAT), qn_ref[...]))
    qq = _dot(rq, wq_ref[...])
    for h in range(MLA_HEADS):
        blk = qq[:, h * MLA_PAD:(h + 1) * MLA_PAD]
        rot = qq[:, (MLA_HEADS + h) * MLA_PAD:(MLA_HEADS + h + 1) * MLA_PAD]
        q_ref[h] = _bf16(blk * tab(T_CQ) + rot * tab(T_SQ))

    rkv = _bf16(_rmsnorm(proj(C_KVLAT, C_CONV), kvn_ref[...]))
    kvv = _dot(rkv, wkv_ref[...])
    kr = proj(C_KR, C_KRR) * tab(T_CK) + proj(C_KRR, C_MAIN) * tab(T_SK)
    kr = pltpu.roll(kr, MLA_NOPE, axis=1)
    ones = jnp.where(lax.broadcasted_iota(jnp.int32, (1, LANES), 1) >= HALF, 1.0, 0.0)
    for h in range(MLA_HEADS):
        sl = slice(h * MLA_PAD, (h + 1) * MLA_PAD)
        k_ref[h] = _bf16(kvv[:, sl] + kr)
        vh = kvv[:, MLA_HEADS * MLA_PAD + h * LANES:MLA_HEADS * MLA_PAD + (h + 1) * LANES] + ones
        v_ref[h] = _bf16(vh.T)

    b_gate = proj(C_CONV, C_CONV + CONV_WIDTH)
    xh = jnp.concatenate([xp_ref[...], xn_ref[...]], axis=0)
    uh = _bf16(_rmsnorm(xh, g))
    cx = _dot(jnp.concatenate([u, uh], axis=0), wm_ref[:, C_CONV + CONV_WIDTH:C_SQ])
    zz = cx[:, :CONV_WIDTH] * cx[:, CONV_WIDTH:]
    z = zz[:TOKEN_TILE]
    zh = zz[TOKEN_TILE:]
    row0 = i * TOKEN_TILE
    has_prev = (row0 % seq) != 0
    has_next = ((row0 + TOKEN_TILE) % seq) != 0
    z_prev = jnp.where(has_prev, zh[HALO - 1:HALO, :], 0.0)
    z_next = jnp.where(has_next, zh[HALO:HALO + 1, :], 0.0)
    rows = lax.broadcasted_iota(jnp.int32, (TOKEN_TILE, CONV_WIDTH), 0)
    z_up = jnp.where(rows == 0, z_prev, pltpu.roll(z, 1, axis=0))
    z_dn = jnp.where(rows == TOKEN_TILE - 1, z_next, pltpu.roll(z, TOKEN_TILE - 1, axis=0))
    y = z_up * cw_ref[0:1, :] + z * cw_ref[1:2, :] + z_dn * cw_ref[2:3, :]
    yb_ref[...] = _bf16(b_gate * y)

    sq = proj(C_SQ, C_SQR)
    sqr = proj(C_SQR, C_SK)
    scale = SWA_HEAD_DIM ** -0.5 * LOG2_E
    for p in range(SWA_HEADS * SWA_HEAD_DIM // LANES):
        sl = slice(p * LANES, (p + 1) * LANES)
        sq_ref[:, sl] = _bf16((sq[:, sl] * tab(T_CS) + sqr[:, sl] * tab(T_SS)) * scale)
    sk_ref[...] = _bf16(proj(C_SK, C_SKR) * tab(T_CS) + proj(C_SKR, C_SV) * tab(T_SS))
    sv_ref[...] = _bf16(proj(C_SV, C_KR))


def _proj(x, tables, lw, layer, seq):
    t = x.shape[0]
    nt = t // TOKEN_TILE
    tiles_per_seq = seq // TOKEN_TILE
    halo_per_tile = TOKEN_TILE // HALO
    n_halo = t // HALO

    def tok(width):
        return pl.BlockSpec((TOKEN_TILE, width), lambda i: (i, 0))

    def out(width):
        return jax.ShapeDtypeStruct((t, width), jnp.bfloat16)

    in_specs = [
        tok(D_MODEL),
        pl.BlockSpec((HALO, D_MODEL), lambda i: (jnp.maximum(i * halo_per_tile - 1, 0), 0)),
        pl.BlockSpec((HALO, D_MODEL), lambda i: (jnp.minimum((i + 1) * halo_per_tile, n_halo - 1), 0)),
        pl.BlockSpec((TOKEN_TILE, 6 * LANES), lambda i: (i % tiles_per_seq, 0)),
        _layer_spec((1, D_MODEL), layer),
        _layer_spec((D_MODEL, C_MAIN), layer),
        _layer_spec((1, MLA_Q_RANK), layer),
        _layer_spec((MLA_Q_RANK, 2 * MLA_HEADS * MLA_PAD), layer),
        _layer_spec((1, MLA_KV_RANK), layer),
        _layer_spec((MLA_KV_RANK, MLA_HEADS * (MLA_PAD + LANES)), layer),
        _layer_spec((HALO, CONV_WIDTH), layer),
    ]
    widths = [CONV_WIDTH, SWA_HEADS * SWA_HEAD_DIM, SWA_KV_HEADS * SWA_HEAD_DIM,
              SWA_KV_HEADS * SWA_HEAD_DIM]
    head_spec = pl.BlockSpec((MLA_HEADS, TOKEN_TILE, LANES), lambda i: (0, i, 0))
    head_shape = jax.ShapeDtypeStruct((MLA_HEADS, t, LANES), jnp.bfloat16)
    return pl.pallas_call(
        functools.partial(_proj_body, seq=seq),
        grid=(nt,),
        in_specs=in_specs,
        out_specs=[head_spec, head_spec, pl.BlockSpec((MLA_HEADS, LANES, TOKEN_TILE), lambda i: (0, 0, i))]
        + [tok(w) for w in widths],
        out_shape=[head_shape, head_shape, jax.ShapeDtypeStruct((MLA_HEADS, LANES, t), jnp.bfloat16)]
        + [out(w) for w in widths],
        compiler_params=_params(("parallel",)),
        name="proj",
    )(x, x, x, tables, lw["mix_norm"], lw["w_main"], lw["q_norm"], lw["w_q"], lw["kv_norm"],
      lw["w_kv"], lw["conv_w"])


def _mla_body(q_ref, k_ref, vt_ref, o_ref, s0_ref, s1_ref, m0_ref, m1_ref, o_scr, *, seq, sub):
    n_chunks = seq // MLA_TK
    item_rows = sub * MLA_TQ
    n_items = (MLA_QBLOCK // item_rows) * MLA_GROUP
    assert n_items % 2 == 0 and n_items >= 4
    low = lax.broadcasted_iota(jnp.int32, (MLA_TQ, LANES), 1) < HALF
    bufs = ((s0_ref, m0_ref), (s1_ref, m1_ref))

    def rows_of(t, u):
        return pl.ds(pl.multiple_of((t // MLA_GROUP) * item_rows + u * MLA_TQ, MLA_TQ), MLA_TQ)

    def pass_a(t, slot):
        s_ref, m_ref = bufs[slot]
        head = t % MLA_GROUP
        for u in range(sub):
            q = q_ref[head, rows_of(t, u), :]
            mm = None
            for c in range(n_chunks):
                s = _dot_nt(k_ref[head, c * MLA_TK:(c + 1) * MLA_TK, :], q)
                s_ref[u, c] = s
                cm = jnp.max(s, axis=0, keepdims=True)
                mm = cm if mm is None else jnp.maximum(mm, cm)
            m_ref[u] = jnp.broadcast_to(mm, (SUBLANES, MLA_TQ))

    def pass_b(t, slot):
        s_ref, m_ref = bufs[slot]
        head = t % MLA_GROUP
        odd = (head % 2) == 1
        valid = jnp.logical_xor(low, odd)
        for u in range(sub):
            m = m_ref[u][:1, :]
            res = None
            for c in range(n_chunks):
                p = _bf16(jnp.exp2(s_ref[u, c] - m))
                pv = _dot(vt_ref[head, :, c * MLA_TK:(c + 1) * MLA_TK], p)
                res = pv if res is None else res + pv
            r = res.T
            val = r / pltpu.roll(r, HALF, axis=1)
            val = jnp.where(odd, pltpu.roll(val, HALF, axis=1), val)
            new = jnp.where(valid, val, jnp.where(odd, o_scr[u], 0.0))
            o_scr[u] = new
            o_ref[head // 2, rows_of(t, u), :] = _bf16(new)

    o_scr[...] = jnp.zeros((sub, MLA_TQ, LANES), jnp.float32)
    pass_a(0, 0)

    def pair(j, carry):
        t = 2 * j
        pass_a(t + 1, 1)
        pass_b(t, 0)
        pass_a(t + 2, 0)
        pass_b(t + 1, 1)
        return carry

    lax.fori_loop(0, n_items // 2 - 1, pair, 0)
    pass_a(n_items - 1, 1)
    pass_b(n_items - 2, 0)
    pass_b(n_items - 1, 1)


def _mla(q, k, vt, batch, seq):
    t = q.shape[1]
    sub = max(1, min(MLA_QBLOCK // MLA_TQ, MLA_S_BYTES // (2 * 4 * seq * MLA_TQ)))
    nqb = seq // MLA_QBLOCK
    groups = MLA_HEADS // MLA_GROUP
    score = pltpu.VMEM((sub, seq // MLA_TK, MLA_TK, MLA_TQ), jnp.float32)
    row_max = pltpu.VMEM((sub, SUBLANES, MLA_TQ), jnp.float32)
    return pl.pallas_call(
        functools.partial(_mla_body, seq=seq, sub=sub),
        grid=(batch, groups, nqb),
        in_specs=[pl.BlockSpec((MLA_GROUP, MLA_QBLOCK, LANES), lambda b, g, i: (g, b * nqb + i, 0)),
                  pl.BlockSpec((MLA_GROUP, seq, LANES), lambda b, g, i: (g, b, 0),
                               pipeline_mode=pl.Buffered(1)),
                  pl.BlockSpec((MLA_GROUP, LANES, seq), lambda b, g, i: (g, 0, b),
                               pipeline_mode=pl.Buffered(1))],
        out_specs=pl.BlockSpec((MLA_GROUP // 2, MLA_QBLOCK, LANES), lambda b, g, i: (g, b * nqb + i, 0)),
        out_shape=jax.ShapeDtypeStruct((MLA_HEADS // 2, t, LANES), jnp.bfloat16),
        scratch_shapes=[score, score, row_max, row_max, pltpu.VMEM((sub, MLA_TQ, LANES), jnp.float32)],
        compiler_params=_params(("parallel", "parallel", "arbitrary")),
        name="mla_attn",
    )(q, k, vt)


def _swa_body(sink_ref, bias_ref, q_ref, kp_ref, kc_ref, kn_ref, vp_ref, vc_ref, vn_ref, o_ref,
              kbuf, vbuf, *, seq):
    w = SWA_WINDOW
    n_sub = SWA_TILE // w
    i = pl.program_id(0)
    kbuf[0:w, :] = kp_ref[...]
    kbuf[w:w + SWA_TILE, :] = kc_ref[...]
    kbuf[w + SWA_TILE:, :] = kn_ref[...]
    vbuf[0:w, :] = vp_ref[...]
    vbuf[w:w + SWA_TILE, :] = vc_ref[...]
    vbuf[w + SWA_TILE:, :] = vn_ref[...]

    rows = SWA_GROUP * w
    head_row = lax.broadcasted_iota(jnp.int32, (rows, 1), 0) // w
    lane_q = lax.broadcasted_iota(jnp.int32, (w, LANES), 1)
    pos0 = (i * SWA_TILE) % seq
    first = jnp.where(pos0 == 0, 1, 0)
    last = jnp.where(pos0 + SWA_TILE == seq, 2, 0)

    for n in range(n_sub):
        bias = bias_ref[first if n == 0 else (last if n == n_sub - 1 else 0)]
        kb = kbuf[n * w:(n + 3) * w, :]
        vb = vbuf[n * w:(n + 3) * w, :]
        outs = [None] * SWA_HEADS
        for g in range(SWA_KV_HEADS):
            keep = (lane_q >= HALF) if g else (lane_q < HALF)
            parts = []
            sink = jnp.zeros((rows, 1), jnp.float32)
            for a in range(SWA_GROUP):
                head = g * SWA_GROUP + a
                blk = q_ref[n * w:(n + 1) * w, (head // 2) * LANES:(head // 2 + 1) * LANES]
                blk = blk.astype(jnp.float32)
                if head % 2 != g:
                    blk = pltpu.roll(blk, HALF, axis=1)
                parts.append(_bf16(jnp.where(keep, blk, 0.0)))
                sink = jnp.where(head_row == a, sink_ref[head] * LOG2_E, sink)
            qs = jnp.concatenate(parts, axis=0)
            s = jnp.minimum(_dot_nt(qs, kb), bias)
            m = jnp.maximum(jnp.max(s, axis=-1, keepdims=True), sink)
            p = jnp.exp2(s - m)
            denom = jnp.sum(p, axis=-1, keepdims=True) + jnp.exp2(sink - m)
            pv = _dot(_bf16(p), vb) * (1.0 / denom)
            for a in range(SWA_GROUP):
                head = g * SWA_GROUP + a
                blk = pv[a * w:(a + 1) * w, :]
                if head % 2 != g:
                    blk = pltpu.roll(blk, HALF, axis=1)
                outs[head] = blk
        for pr in range(SWA_HEADS // 2):
            o_ref[n * w:(n + 1) * w, pr * LANES:(pr + 1) * LANES] = _bf16(
                jnp.where(lane_q < HALF, outs[2 * pr], outs[2 * pr + 1]))


def _swa_bias():
    w = SWA_WINDOW
    rows = SWA_GROUP * w
    r = jnp.arange(rows)[:, None] % w
    c = jnp.arange(3 * w)[None, :]
    band = jnp.abs(c - w - r) <= SWA_WINDOW
    visible = jnp.stack([band, band & (c >= w), band & (c < 2 * w)])
    return jnp.where(visible, jnp.inf, MASK_VALUE).astype(jnp.float32)


def _swa(sink, q, k, v, seq):
    t = q.shape[0]
    w = SWA_WINDOW
    per_tile = SWA_TILE // w
    assert per_tile >= 2
    n_blocks = t // w
    kvw = SWA_KV_HEADS * SWA_HEAD_DIM

    prev = pl.BlockSpec((w, kvw), lambda i: (jnp.maximum(i * per_tile - 1, 0), 0))
    cur = pl.BlockSpec((SWA_TILE, kvw), lambda i: (i, 0))
    nxt = pl.BlockSpec((w, kvw), lambda i: (jnp.minimum((i + 1) * per_tile, n_blocks - 1), 0))
    qspec = pl.BlockSpec((SWA_TILE, SWA_HEADS * SWA_HEAD_DIM), lambda i: (i, 0))
    return pl.pallas_call(
        functools.partial(_swa_body, seq=seq),
        grid=(t // SWA_TILE,),
        in_specs=[pl.BlockSpec(memory_space=pltpu.SMEM), _const_spec((3, SWA_GROUP * w, 3 * w)),
                  qspec, prev, cur, nxt, prev, cur, nxt],
        out_specs=qspec,
        out_shape=jax.ShapeDtypeStruct((t, SWA_HEADS * SWA_HEAD_DIM), jnp.bfloat16),
        scratch_shapes=[pltpu.VMEM((SWA_TILE + 2 * w, kvw), jnp.bfloat16),
                        pltpu.VMEM((SWA_TILE + 2 * w, kvw), jnp.bfloat16)],
        compiler_params=_params(("parallel",)),
        name="swa_attn",
    )(sink, _swa_bias(), q, k, k, k, v, v, v)


def _merge_body(x_ref, oa_ref, yb_ref, oc_ref, g_ref, wg_ref, wa_ref, wb_ref, wc_ref, wo_ref, o_ref):
    x = x_ref[...]
    u = _bf16(_rmsnorm(x, g_ref[...]))
    oa = jnp.concatenate([oa_ref[p] for p in range(MLA_HEADS // 2)], axis=1)
    branches = ((oa, wa_ref), (yb_ref[...], wb_ref), (oc_ref[...], wc_ref))
    merged = None
    for b, (y, w_ref) in enumerate(branches):
        gate = _sigmoid(_dot(u, wg_ref[:, b * D_MODEL:(b + 1) * D_MODEL]))
        term = gate * _dot(y, w_ref[...])
        merged = term if merged is None else merged + term
    o_ref[...] = x + _dot(_bf16(merged), wo_ref[...])


def _merge(x, oa, yb, oc, lw, layer):
    t = x.shape[0]

    def tok(width):
        return pl.BlockSpec((TOKEN_TILE, width), lambda i: (i, 0))

    half = MLA_HEADS * MLA_V
    return pl.pallas_call(
        _merge_body,
        grid=(t // TOKEN_TILE,),
        in_specs=[tok(D_MODEL), pl.BlockSpec((MLA_HEADS // 2, TOKEN_TILE, LANES), lambda i: (0, i, 0)),
                  tok(CONV_WIDTH), tok(SWA_HEADS * SWA_HEAD_DIM),
                  _layer_spec((1, D_MODEL), layer), _layer_spec((D_MODEL, 3 * D_MODEL), layer),
                  _layer_spec((half, D_MODEL), layer), _layer_spec((CONV_WIDTH, D_MODEL), layer),
                  _layer_spec((SWA_HEADS * SWA_HEAD_DIM, D_MODEL), layer),
                  _layer_spec((D_MODEL, D_MODEL), layer)],
        out_specs=tok(D_MODEL),
        out_shape=jax.ShapeDtypeStruct((t, D_MODEL), jnp.float32),
        compiler_params=_params(("parallel",)),
        name="merge",
    )(x, oa, yb, oc, lw["mix_norm"], lw["w_gate"], lw["w_a"], lw["w_b"], lw["w_c"], lw["w_o"])


def _rotate_half_cols(w, head_dim):
    lead = w.shape[:-1]
    wh = w.reshape(*lead, -1, 2, head_dim // 2)
    return jnp.concatenate([-wh[..., 1:2, :], wh[..., 0:1, :]], axis=-2).reshape(w.shape)


def _pad_cols(w, width):
    return jnp.pad(w, [(0, 0)] * (w.ndim - 1) + [(0, width - w.shape[-1])])


def _prepare_weights(w_in, mla_w_uq, mla_w_ukv):
    depth = w_in.shape[0]
    o = 0
    q_lat = w_in[..., o:o + MLA_Q_RANK]; o += MLA_Q_RANK
    kv_lat = w_in[..., o:o + MLA_KV_RANK]; o += MLA_KV_RANK
    k_rope = w_in[..., o:o + MLA_ROPE]; o += MLA_ROPE
    conv = w_in[..., o:o + 3 * CONV_WIDTH]; o += 3 * CONV_WIDTH
    sq = w_in[..., o:o + SWA_HEADS * SWA_HEAD_DIM]; o += SWA_HEADS * SWA_HEAD_DIM
    sk = w_in[..., o:o + SWA_KV_HEADS * SWA_HEAD_DIM]; o += SWA_KV_HEADS * SWA_HEAD_DIM
    sv = w_in[..., o:o + SWA_KV_HEADS * SWA_HEAD_DIM]; o += SWA_KV_HEADS * SWA_HEAD_DIM
    w_gate = w_in[..., o:]
    w_main = jnp.concatenate([
        q_lat, kv_lat, conv, sq, _rotate_half_cols(sq, SWA_HEAD_DIM), sk,
        _rotate_half_cols(sk, SWA_HEAD_DIM), sv, _pad_cols(k_rope, LANES),
        _pad_cols(_rotate_half_cols(k_rope, MLA_ROPE), LANES)], axis=-1)

    uq = mla_w_uq.reshape(depth, MLA_Q_RANK, MLA_HEADS, MLA_NOPE + MLA_ROPE)
    nope, rope = uq[..., :MLA_NOPE], uq[..., MLA_NOPE:]
    zeros = jnp.zeros_like
    pad = zeros(rope)
    w_q = jnp.concatenate([
        jnp.concatenate([nope, rope, pad], axis=-1).reshape(depth, MLA_Q_RANK, -1),
        jnp.concatenate([zeros(nope), _rotate_half_cols(rope, MLA_ROPE), pad], axis=-1)
        .reshape(depth, MLA_Q_RANK, -1)], axis=-1)

    ukv = mla_w_ukv.reshape(depth, MLA_KV_RANK, MLA_HEADS, MLA_NOPE + MLA_V)
    k_nope, v = ukv[..., :MLA_NOPE], ukv[..., MLA_NOPE:]
    v_pad = jnp.concatenate([v, zeros(v)], axis=-1)
    w_kv = jnp.concatenate([
        jnp.concatenate([k_nope, zeros(k_nope)], axis=-1).reshape(depth, MLA_KV_RANK, -1),
        v_pad.reshape(depth, MLA_KV_RANK, -1)], axis=-1)
    return w_main, w_gate, w_q, w_kv


def _rope_tables(seq):
    def cos_sin(dim):
        inv = 1.0 / (ROPE_THETA ** (jnp.arange(0, dim, 2, dtype=jnp.float32) / dim))
        ang = jnp.arange(seq, dtype=jnp.float32)[:, None] * inv[None, :]
        return jnp.cos(ang), jnp.sin(ang)

    mc, ms = cos_sin(MLA_ROPE)
    sc, ss = cos_sin(SWA_HEAD_DIM)
    scale = (MLA_NOPE + MLA_ROPE) ** -0.5 * LOG2_E
    ones = jnp.ones((seq, MLA_NOPE), jnp.float32)
    zq = jnp.zeros((seq, MLA_PAD - MLA_NOPE - MLA_ROPE), jnp.float32)
    zn = jnp.zeros((seq, MLA_NOPE), jnp.float32)
    zk = jnp.zeros((seq, LANES - MLA_ROPE), jnp.float32)
    cols = [
        scale * jnp.concatenate([ones, mc, mc, zq], axis=1),
        scale * jnp.concatenate([zn, ms, ms, zq], axis=1),
        jnp.concatenate([mc, mc, zk], axis=1),
        jnp.concatenate([ms, ms, zk], axis=1),
        jnp.concatenate([sc, sc, sc, sc], axis=1),
        jnp.concatenate([ss, ss, ss, ss], axis=1),
    ]
    return jnp.concatenate(cols, axis=1)


def _encode(x, lw, depth, final_norm):
    batch, seq, _ = x.shape
    tables = _rope_tables(seq)
    h = x.reshape(batch * seq, D_MODEL)
    for layer in range(depth):
        h = _ffn(h, lw["ffn1_norm"], lw["ffn1_w_gu"], lw["ffn1_w_down"], final_norm, layer, False)
        q, k, v, yb, sq, sk, sv = _proj(h, tables, lw, layer, seq)
        oa = _mla(q, k, v, batch, seq)
        oc = _swa(lw["sink"][layer], sq, sk, sv, seq)
        h = _merge(h, oa, yb, oc, lw, layer)
        h = _ffn(h, lw["ffn2_norm"], lw["ffn2_w_gu"], lw["ffn2_w_down"], final_norm, layer,
                 layer == depth - 1)
    return h.reshape(batch, seq, D_MODEL)


def kernel(x_prompt, x_sample, ffn1_norm, ffn1_w_gu, ffn1_w_down, mix_norm, w_in, mla_q_norm,
           mla_w_uq, mla_kv_norm, mla_w_ukv, mla_w_o, conv_w, conv_w_o, swa_sink, swa_w_o, w_o,
           ffn2_norm, ffn2_w_gu, ffn2_w_down, final_norm):
    depth = w_in.shape[0]
    w_main, w_gate, w_q, w_kv = _prepare_weights(_bf16(w_in), _bf16(mla_w_uq), _bf16(mla_w_ukv))
    lw = {
        "ffn1_norm": ffn1_norm[:, None, :], "ffn1_w_gu": _bf16(ffn1_w_gu), "ffn1_w_down": _bf16(ffn1_w_down),
        "ffn2_norm": ffn2_norm[:, None, :], "ffn2_w_gu": _bf16(ffn2_w_gu), "ffn2_w_down": _bf16(ffn2_w_down),
        "mix_norm": mix_norm[:, None, :], "w_main": w_main, "w_gate": w_gate,
        "q_norm": mla_q_norm[:, None, :], "w_q": w_q, "kv_norm": mla_kv_norm[:, None, :], "w_kv": w_kv,
        "conv_w": jnp.pad(conv_w, ((0, 0), (0, HALO - conv_w.shape[1]), (0, 0))), "sink": swa_sink,
        "w_a": _bf16(mla_w_o), "w_b": _bf16(conv_w_o), "w_c": _bf16(swa_w_o), "w_o": _bf16(w_o),
    }
    fin = final_norm[None, :]
    return (_encode(x_prompt, lw, depth, fin), _encode(x_sample, lw, depth, fin))
```

```python
import functools

import jax
import jax.numpy as jnp
from jax import lax
from jax.experimental import pallas as pl
from jax.experimental.pallas import tpu as pltpu

D_MODEL = 1024
D_FF = 2816
NORM_EPS = 1e-6
ROPE_THETA = 10000.0

MLA_HEADS = 8
MLA_Q_RANK = 256
MLA_KV_RANK = 128
MLA_NOPE = 64
MLA_ROPE = 32
MLA_V = 64

CONV_WIDTH = 512

SWA_HEADS = 8
SWA_KV_HEADS = 2
SWA_HEAD_DIM = 64
SWA_WINDOW = 128
SWA_GROUP = SWA_HEADS // SWA_KV_HEADS

LANES = 128
HALF = LANES // 2
MLA_PAD = LANES
MASK_VALUE = -1e30
LOG2_E = 1.4426950408889634

TOKEN_TILE = 512
FF_CHUNK = 1408
MLA_TQ = 256
MLA_S_BYTES = 16 * 1024 * 1024
MLA_TK = 2048
MLA_QBLOCK = 4096
MLA_GROUP = 4
SWA_TILE = 512
SUBLANES = 8
HALO = SUBLANES
VMEM_LIMIT = 56 * 1024 * 1024

C_QLAT = 0
C_KVLAT = C_QLAT + MLA_Q_RANK
C_CONV = C_KVLAT + MLA_KV_RANK
C_SQ = C_CONV + 3 * CONV_WIDTH
C_SK = C_SQ + SWA_HEADS * SWA_HEAD_DIM
C_SV = C_SK + SWA_KV_HEADS * SWA_HEAD_DIM
C_KR = C_SV + SWA_KV_HEADS * SWA_HEAD_DIM
C_MAIN = C_KR + LANES

T_CQ, T_SQA, T_SQB, T_CK, T_SKA, T_SKB, T_CS, T_SSA, T_SSB = range(9)
N_TABLES = 9


def _bf16(x):
    return x.astype(jnp.bfloat16)


def _dot(a, b):
    return jnp.dot(a, b, preferred_element_type=jnp.float32)


def _dot_nt(a, b):
    return lax.dot_general(a, b, (((1,), (1,)), ((), ())), preferred_element_type=jnp.float32)


def _rmsnorm(x, g):
    return x * lax.rsqrt(jnp.mean(x * x, axis=-1, keepdims=True) + NORM_EPS) * g


def _sigmoid(x):
    return 1.0 / (1.0 + jnp.exp(-x))


def _const_spec(shape):
    return pl.BlockSpec(shape, lambda *_: (0,) * len(shape), pipeline_mode=pl.Buffered(1))


def _layer_spec(shape, layer):
    return pl.BlockSpec((None,) + shape, lambda *_: (layer,) + (0,) * len(shape),
                        pipeline_mode=pl.Buffered(1))


def _params(sem):
    return pltpu.CompilerParams(dimension_semantics=sem, vmem_limit_bytes=VMEM_LIMIT)


def _ffn_body(x_ref, g_ref, wgu_ref, wd_ref, fin_ref, o_ref, h_ref, *, final):
    x = x_ref[...]
    xn = _bf16(_rmsnorm(x, g_ref[...]))
    for c in range(D_FF // FF_CHUNK):
        a = _dot(xn, wgu_ref[:, c * FF_CHUNK:(c + 1) * FF_CHUNK])
        b = _dot(xn, wgu_ref[:, D_FF + c * FF_CHUNK:D_FF + (c + 1) * FF_CHUNK])
        h_ref[:, c * FF_CHUNK:(c + 1) * FF_CHUNK] = _bf16(a * _sigmoid(a) * b)
    y = x + 0.5 * _dot(h_ref[...], wd_ref[...])
    if final:
        y = _rmsnorm(y, fin_ref[...])
    o_ref[...] = y


def _ffn(x, g, wgu, wd, fin, layer, final):
    t = x.shape[0]
    tok = pl.BlockSpec((TOKEN_TILE, D_MODEL), lambda i: (i, 0))
    return pl.pallas_call(
        functools.partial(_ffn_body, final=final),
        grid=(t // TOKEN_TILE,),
        in_specs=[tok, _layer_spec((1, D_MODEL), layer), _layer_spec((D_MODEL, 2 * D_FF), layer),
                  _layer_spec((D_FF, D_MODEL), layer), _const_spec((1, D_MODEL))],
        out_specs=tok,
        out_shape=jax.ShapeDtypeStruct((t, D_MODEL), jnp.float32),
        scratch_shapes=[pltpu.VMEM((TOKEN_TILE, D_FF), jnp.bfloat16)],
        compiler_params=_params(("parallel",)),
        name="ffn_final" if final else "ffn",
    )(x, g, wgu, wd, fin)


def _proj_body(x_ref, xp_ref, xn_ref, tab_ref, g_ref, wm_ref, qn_ref, wq_ref, kvn_ref, wkv_ref,
               cw_ref, q_ref, k_ref, v_ref, yb_ref, sq_ref, sk_ref, sv_ref, *, seq):
    i = pl.program_id(0)
    g = g_ref[...]
    u = _bf16(_rmsnorm(x_ref[...], g))

    def tab(j):
        return tab_ref[:, j * LANES:(j + 1) * LANES]

    def proj(c0, c1):
        return _dot(u, wm_ref[:, c0:c1])

    def rope(x, cos, sin_a, sin_b, half):
        return (x * tab(cos) + pltpu.roll(x, LANES - half, axis=1) * tab(sin_a)
                + pltpu.roll(x, half, axis=1) * tab(sin_b))

    rq = _bf16(_rmsnorm(proj(C_QLAT, C_KVLAT), qn_ref[...]))
    qq = _dot(rq, wq_ref[...])
    for h in range(MLA_HEADS):
        q_ref[h] = _bf16(rope(qq[:, h * MLA_PAD:(h + 1) * MLA_PAD], T_CQ, T_SQA, T_SQB, MLA_ROPE // 2))

    rkv = _bf16(_rmsnorm(proj(C_KVLAT, C_CONV), kvn_ref[...]))
    kvv = _dot(rkv, wkv_ref[...])
    kr = rope(proj(C_KR, C_MAIN), T_CK, T_SKA, T_SKB, MLA_ROPE // 2)
    kr = pltpu.roll(kr, MLA_NOPE, axis=1)
    ones = jnp.where(lax.broadcasted_iota(jnp.int32, (1, LANES), 1) >= HALF, 1.0, 0.0)
    for h in range(MLA_HEADS):
        sl = slice(h * MLA_PAD, (h + 1) * MLA_PAD)
        k_ref[h] = _bf16(kvv[:, sl] + kr)
        vh = kvv[:, MLA_HEADS * MLA_PAD + h * LANES:MLA_HEADS * MLA_PAD + (h + 1) * LANES] + ones
        v_ref[h] = _bf16(vh.T)

    b_gate = proj(C_CONV, C_CONV + CONV_WIDTH)
    xh = jnp.concatenate([xp_ref[...], xn_ref[...]], axis=0)
    uh = _bf16(_rmsnorm(xh, g))
    cx = _dot(jnp.concatenate([u, uh], axis=0), wm_ref[:, C_CONV + CONV_WIDTH:C_SQ])
    zz = cx[:, :CONV_WIDTH] * cx[:, CONV_WIDTH:]
    z = zz[:TOKEN_TILE]
    zh = zz[TOKEN_TILE:]
    row0 = i * TOKEN_TILE
    has_prev = (row0 % seq) != 0
    has_next = ((row0 + TOKEN_TILE) % seq) != 0
    z_prev = jnp.where(has_prev, zh[HALO - 1:HALO, :], 0.0)
    z_next = jnp.where(has_next, zh[HALO:HALO + 1, :], 0.0)
    rows = lax.broadcasted_iota(jnp.int32, (TOKEN_TILE, CONV_WIDTH), 0)
    z_up = jnp.where(rows == 0, z_prev, pltpu.roll(z, 1, axis=0))
    z_dn = jnp.where(rows == TOKEN_TILE - 1, z_next, pltpu.roll(z, TOKEN_TILE - 1, axis=0))
    y = z_up * cw_ref[0:1, :] + z * cw_ref[1:2, :] + z_dn * cw_ref[2:3, :]
    yb_ref[...] = _bf16(b_gate * y)

    sq = proj(C_SQ, C_SK)
    scale = SWA_HEAD_DIM ** -0.5 * LOG2_E
    for p in range(SWA_HEADS * SWA_HEAD_DIM // LANES):
        sl = slice(p * LANES, (p + 1) * LANES)
        sq_ref[:, sl] = _bf16(rope(sq[:, sl], T_CS, T_SSA, T_SSB, SWA_HEAD_DIM // 2) * scale)
    sk_ref[...] = _bf16(rope(proj(C_SK, C_SV), T_CS, T_SSA, T_SSB, SWA_HEAD_DIM // 2))
    sv_ref[...] = _bf16(proj(C_SV, C_KR))


def _proj(x, tables, lw, layer, seq):
    t = x.shape[0]
    nt = t // TOKEN_TILE
    tiles_per_seq = seq // TOKEN_TILE
    halo_per_tile = TOKEN_TILE // HALO
    n_halo = t // HALO

    def tok(width):
        return pl.BlockSpec((TOKEN_TILE, width), lambda i: (i, 0))

    def out(width):
        return jax.ShapeDtypeStruct((t, width), jnp.bfloat16)

    in_specs = [
        tok(D_MODEL),
        pl.BlockSpec((HALO, D_MODEL), lambda i: (jnp.maximum(i * halo_per_tile - 1, 0), 0)),
        pl.BlockSpec((HALO, D_MODEL), lambda i: (jnp.minimum((i + 1) * halo_per_tile, n_halo - 1), 0)),
        pl.BlockSpec((TOKEN_TILE, N_TABLES * LANES), lambda i: (i % tiles_per_seq, 0)),
        _layer_spec((1, D_MODEL), layer),
        _layer_spec((D_MODEL, C_MAIN), layer),
        _layer_spec((1, MLA_Q_RANK), layer),
        _layer_spec((MLA_Q_RANK, MLA_HEADS * MLA_PAD), layer),
        _layer_spec((1, MLA_KV_RANK), layer),
        _layer_spec((MLA_KV_RANK, MLA_HEADS * (MLA_PAD + LANES)), layer),
        _layer_spec((HALO, CONV_WIDTH), layer),
    ]
    widths = [CONV_WIDTH, SWA_HEADS * SWA_HEAD_DIM, SWA_KV_HEADS * SWA_HEAD_DIM,
              SWA_KV_HEADS * SWA_HEAD_DIM]
    head_spec = pl.BlockSpec((MLA_HEADS, TOKEN_TILE, LANES), lambda i: (0, i, 0))
    head_shape = jax.ShapeDtypeStruct((MLA_HEADS, t, LANES), jnp.bfloat16)
    return pl.pallas_call(
        functools.partial(_proj_body, seq=seq),
        grid=(nt,),
        in_specs=in_specs,
        out_specs=[head_spec, head_spec, pl.BlockSpec((MLA_HEADS, LANES, TOKEN_TILE), lambda i: (0, 0, i))]
        + [tok(w) for w in widths],
        out_shape=[head_shape, head_shape, jax.ShapeDtypeStruct((MLA_HEADS, LANES, t), jnp.bfloat16)]
        + [out(w) for w in widths],
        compiler_params=_params(("parallel",)),
        name="proj",
    )(x, x, x, tables, lw["mix_norm"], lw["w_main"], lw["q_norm"], lw["w_q"], lw["kv_norm"],
      lw["w_kv"], lw["conv_w"])


def _mla_body(q_ref, k_ref, vt_ref, o_ref, s0_ref, s1_ref, m0_ref, m1_ref, o_scr, *, seq, sub):
    n_chunks = seq // MLA_TK
    item_rows = sub * MLA_TQ
    n_items = (MLA_QBLOCK // item_rows) * MLA_GROUP
    assert n_items % 2 == 0 and n_items >= 4
    low = lax.broadcasted_iota(jnp.int32, (MLA_TQ, LANES), 1) < HALF
    bufs = ((s0_ref, m0_ref), (s1_ref, m1_ref))

    def rows_of(t, u):
        return pl.ds(pl.multiple_of((t // MLA_GROUP) * item_rows + u * MLA_TQ, MLA_TQ), MLA_TQ)

    def pass_a(t, slot):
        s_ref, m_ref = bufs[slot]
        head = t % MLA_GROUP
        for u in range(sub):
            q = q_ref[head, rows_of(t, u), :]
            mm = None
            for c in range(n_chunks):
                s = _dot_nt(k_ref[head, c * MLA_TK:(c + 1) * MLA_TK, :], q)
                s_ref[u, c] = s
                cm = jnp.max(s, axis=0, keepdims=True)
                mm = cm if mm is None else jnp.maximum(mm, cm)
            m_ref[u] = jnp.broadcast_to(mm, (SUBLANES, MLA_TQ))

    def pass_b(t, slot):
        s_ref, m_ref = bufs[slot]
        head = t % MLA_GROUP
        odd = (head % 2) == 1
        valid = jnp.logical_xor(low, odd)
        for u in range(sub):
            m = m_ref[u][:1, :]
            res = None
            for c in range(n_chunks):
                p = _bf16(jnp.exp2(s_ref[u, c] - m))
                pv = _dot(vt_ref[head, :, c * MLA_TK:(c + 1) * MLA_TK], p)
                res = pv if res is None else res + pv
            r = res.T
            val = r / pltpu.roll(r, HALF, axis=1)
            val = jnp.where(odd, pltpu.roll(val, HALF, axis=1), val)
            new = jnp.where(valid, val, jnp.where(odd, o_scr[u], 0.0))
            o_scr[u] = new
            o_ref[head // 2, rows_of(t, u), :] = _bf16(new)

    o_scr[...] = jnp.zeros((sub, MLA_TQ, LANES), jnp.float32)
    pass_a(0, 0)

    def pair(j, carry):
        t = 2 * j
        pass_a(t + 1, 1)
        pass_b(t, 0)
        pass_a(t + 2, 0)
        pass_b(t + 1, 1)
        return carry

    lax.fori_loop(0, n_items // 2 - 1, pair, 0)
    pass_a(n_items - 1, 1)
    pass_b(n_items - 2, 0)
    pass_b(n_items - 1, 1)


def _mla(q, k, vt, batch, seq):
    t = q.shape[1]
    sub = max(1, min(MLA_QBLOCK // MLA_TQ, MLA_S_BYTES // (2 * 4 * seq * MLA_TQ)))
    nqb = seq // MLA_QBLOCK
    groups = MLA_HEADS // MLA_GROUP
    score = pltpu.VMEM((sub, seq // MLA_TK, MLA_TK, MLA_TQ), jnp.float32)
    row_max = pltpu.VMEM((sub, SUBLANES, MLA_TQ), jnp.float32)
    return pl.pallas_call(
        functools.partial(_mla_body, seq=seq, sub=sub),
        grid=(batch, groups, nqb),
        in_specs=[pl.BlockSpec((MLA_GROUP, MLA_QBLOCK, LANES), lambda b, g, i: (g, b * nqb + i, 0)),
                  pl.BlockSpec((MLA_GROUP, seq, LANES), lambda b, g, i: (g, b, 0),
                               pipeline_mode=pl.Buffered(1)),
                  pl.BlockSpec((MLA_GROUP, LANES, seq), lambda b, g, i: (g, 0, b),
                               pipeline_mode=pl.Buffered(1))],
        out_specs=pl.BlockSpec((MLA_GROUP // 2, MLA_QBLOCK, LANES), lambda b, g, i: (g, b * nqb + i, 0)),
        out_shape=jax.ShapeDtypeStruct((MLA_HEADS // 2, t, LANES), jnp.bfloat16),
        scratch_shapes=[score, score, row_max, row_max, pltpu.VMEM((sub, MLA_TQ, LANES), jnp.float32)],
        compiler_params=_params(("parallel", "parallel", "arbitrary")),
        name="mla_attn",
    )(q, k, vt)


def _swa_body(sink_ref, bias_ref, q_ref, kp_ref, kc_ref, kn_ref, vp_ref, vc_ref, vn_ref, o_ref,
              kbuf, vbuf, *, seq):
    w = SWA_WINDOW
    n_sub = SWA_TILE // w
    i = pl.program_id(0)
    kbuf[0:w, :] = kp_ref[...]
    kbuf[w:w + SWA_TILE, :] = kc_ref[...]
    kbuf[w + SWA_TILE:, :] = kn_ref[...]
    vbuf[0:w, :] = vp_ref[...]
    vbuf[w:w + SWA_TILE, :] = vc_ref[...]
    vbuf[w + SWA_TILE:, :] = vn_ref[...]

    rows = SWA_GROUP * w
    head_row = lax.broadcasted_iota(jnp.int32, (rows, 1), 0) // w
    lane_q = lax.broadcasted_iota(jnp.int32, (w, LANES), 1)
    pos0 = (i * SWA_TILE) % seq
    first = jnp.where(pos0 == 0, 1, 0)
    last = jnp.where(pos0 + SWA_TILE == seq, 2, 0)

    for n in range(n_sub):
        bias = bias_ref[first if n == 0 else (last if n == n_sub - 1 else 0)]
        kb = kbuf[n * w:(n + 3) * w, :]
        vb = vbuf[n * w:(n + 3) * w, :]
        outs = [None] * SWA_HEADS
        for g in range(SWA_KV_HEADS):
            keep = (lane_q >= HALF) if g else (lane_q < HALF)
            parts = []
            sink = jnp.zeros((rows, 1), jnp.float32)
            for a in range(SWA_GROUP):
                head = g * SWA_GROUP + a
                blk = q_ref[n * w:(n + 1) * w, (head // 2) * LANES:(head // 2 + 1) * LANES]
                blk = blk.astype(jnp.float32)
                if head % 2 != g:
                    blk = pltpu.roll(blk, HALF, axis=1)
                parts.append(_bf16(jnp.where(keep, blk, 0.0)))
                sink = jnp.where(head_row == a, sink_ref[head] * LOG2_E, sink)
            qs = jnp.concatenate(parts, axis=0)
            s = jnp.minimum(_dot_nt(qs, kb), bias)
            m = jnp.maximum(jnp.max(s, axis=-1, keepdims=True), sink)
            p = jnp.exp2(s - m)
            denom = jnp.sum(p, axis=-1, keepdims=True) + jnp.exp2(sink - m)
            pv = _dot(_bf16(p), vb) * (1.0 / denom)
            for a in range(SWA_GROUP):
                head = g * SWA_GROUP + a
                blk = pv[a * w:(a + 1) * w, :]
                if head % 2 != g:
                    blk = pltpu.roll(blk, HALF, axis=1)
                outs[head] = blk
        for pr in range(SWA_HEADS // 2):
            o_ref[n * w:(n + 1) * w, pr * LANES:(pr + 1) * LANES] = _bf16(
                jnp.where(lane_q < HALF, outs[2 * pr], outs[2 * pr + 1]))


def _swa_bias():
    w = SWA_WINDOW
    rows = SWA_GROUP * w
    r = jnp.arange(rows)[:, None] % w
    c = jnp.arange(3 * w)[None, :]
    band = jnp.abs(c - w - r) <= SWA_WINDOW
    visible = jnp.stack([band, band & (c >= w), band & (c < 2 * w)])
    return jnp.where(visible, jnp.inf, MASK_VALUE).astype(jnp.float32)


def _swa(sink, q, k, v, seq):
    t = q.shape[0]
    w = SWA_WINDOW
    per_tile = SWA_TILE // w
    assert per_tile >= 2
    n_blocks = t // w
    kvw = SWA_KV_HEADS * SWA_HEAD_DIM

    prev = pl.BlockSpec((w, kvw), lambda i: (jnp.maximum(i * per_tile - 1, 0), 0))
    cur = pl.BlockSpec((SWA_TILE, kvw), lambda i: (i, 0))
    nxt = pl.BlockSpec((w, kvw), lambda i: (jnp.minimum((i + 1) * per_tile, n_blocks - 1), 0))
    qspec = pl.BlockSpec((SWA_TILE, SWA_HEADS * SWA_HEAD_DIM), lambda i: (i, 0))
    return pl.pallas_call(
        functools.partial(_swa_body, seq=seq),
        grid=(t // SWA_TILE,),
        in_specs=[pl.BlockSpec(memory_space=pltpu.SMEM), _const_spec((3, SWA_GROUP * w, 3 * w)),
                  qspec, prev, cur, nxt, prev, cur, nxt],
        out_specs=qspec,
        out_shape=jax.ShapeDtypeStruct((t, SWA_HEADS * SWA_HEAD_DIM), jnp.bfloat16),
        scratch_shapes=[pltpu.VMEM((SWA_TILE + 2 * w, kvw), jnp.bfloat16),
                        pltpu.VMEM((SWA_TILE + 2 * w, kvw), jnp.bfloat16)],
        compiler_params=_params(("parallel",)),
        name="swa_attn",
    )(sink, _swa_bias(), q, k, k, k, v, v, v)


def _merge_body(x_ref, oa_ref, yb_ref, oc_ref, g_ref, wg_ref, wa_ref, wb_ref, wc_ref, wo_ref, o_ref):
    x = x_ref[...]
    u = _bf16(_rmsnorm(x, g_ref[...]))
    oa = jnp.concatenate([oa_ref[p] for p in range(MLA_HEADS // 2)], axis=1)
    branches = ((oa, wa_ref), (yb_ref[...], wb_ref), (oc_ref[...], wc_ref))
    merged = None
    for b, (y, w_ref) in enumerate(branches):
        gate = _sigmoid(_dot(u, wg_ref[:, b * D_MODEL:(b + 1) * D_MODEL]))
        term = gate * _dot(y, w_ref[...])
        merged = term if merged is None else merged + term
    o_ref[...] = x + _dot(_bf16(merged), wo_ref[...])


def _merge(x, oa, yb, oc, lw, layer):
    t = x.shape[0]

    def tok(width):
        return pl.BlockSpec((TOKEN_TILE, width), lambda i: (i, 0))

    half = MLA_HEADS * MLA_V
    return pl.pallas_call(
        _merge_body,
        grid=(t // TOKEN_TILE,),
        in_specs=[tok(D_MODEL), pl.BlockSpec((MLA_HEADS // 2, TOKEN_TILE, LANES), lambda i: (0, i, 0)),
                  tok(CONV_WIDTH), tok(SWA_HEADS * SWA_HEAD_DIM),
                  _layer_spec((1, D_MODEL), layer), _layer_spec((D_MODEL, 3 * D_MODEL), layer),
                  _layer_spec((half, D_MODEL), layer), _layer_spec((CONV_WIDTH, D_MODEL), layer),
                  _layer_spec((SWA_HEADS * SWA_HEAD_DIM, D_MODEL), layer),
                  _layer_spec((D_MODEL, D_MODEL), layer)],
        out_specs=tok(D_MODEL),
        out_shape=jax.ShapeDtypeStruct((t, D_MODEL), jnp.float32),
        compiler_params=_params(("parallel",)),
        name="merge",
    )(x, oa, yb, oc, lw["mix_norm"], lw["w_gate"], lw["w_a"], lw["w_b"], lw["w_c"], lw["w_o"])


def _pad_cols(w, width):
    return jnp.pad(w, [(0, 0)] * (w.ndim - 1) + [(0, width - w.shape[-1])])


def _prepare_weights(w_in, mla_w_uq, mla_w_ukv):
    depth = w_in.shape[0]
    o = 0
    q_lat = w_in[..., o:o + MLA_Q_RANK]; o += MLA_Q_RANK
    kv_lat = w_in[..., o:o + MLA_KV_RANK]; o += MLA_KV_RANK
    k_rope = w_in[..., o:o + MLA_ROPE]; o += MLA_ROPE
    conv = w_in[..., o:o + 3 * CONV_WIDTH]; o += 3 * CONV_WIDTH
    sq = w_in[..., o:o + SWA_HEADS * SWA_HEAD_DIM]; o += SWA_HEADS * SWA_HEAD_DIM
    sk = w_in[..., o:o + SWA_KV_HEADS * SWA_HEAD_DIM]; o += SWA_KV_HEADS * SWA_HEAD_DIM
    sv = w_in[..., o:o + SWA_KV_HEADS * SWA_HEAD_DIM]; o += SWA_KV_HEADS * SWA_HEAD_DIM
    w_gate = w_in[..., o:]
    w_main = jnp.concatenate([q_lat, kv_lat, conv, sq, sk, sv, _pad_cols(k_rope, LANES)], axis=-1)

    uq = mla_w_uq.reshape(depth, MLA_Q_RANK, MLA_HEADS, MLA_NOPE + MLA_ROPE)
    nope, rope = uq[..., :MLA_NOPE], uq[..., MLA_NOPE:]
    zeros = jnp.zeros_like
    pad = zeros(rope)
    w_q = jnp.concatenate([nope, rope, pad], axis=-1).reshape(depth, MLA_Q_RANK, -1)

    ukv = mla_w_ukv.reshape(depth, MLA_KV_RANK, MLA_HEADS, MLA_NOPE + MLA_V)
    k_nope, v = ukv[..., :MLA_NOPE], ukv[..., MLA_NOPE:]
    v_pad = jnp.concatenate([v, zeros(v)], axis=-1)
    w_kv = jnp.concatenate([
        jnp.concatenate([k_nope, zeros(k_nope)], axis=-1).reshape(depth, MLA_KV_RANK, -1),
        v_pad.reshape(depth, MLA_KV_RANK, -1)], axis=-1)
    return w_main, w_gate, w_q, w_kv


def _rope_tables(seq):
    def cos_sin(dim):
        inv = 1.0 / (ROPE_THETA ** (jnp.arange(0, dim, 2, dtype=jnp.float32) / dim))
        ang = jnp.arange(seq, dtype=jnp.float32)[:, None] * inv[None, :]
        return jnp.cos(ang), jnp.sin(ang)

    mc, ms = cos_sin(MLA_ROPE)
    sc, ss = cos_sin(SWA_HEAD_DIM)
    scale = (MLA_NOPE + MLA_ROPE) ** -0.5 * LOG2_E

    def lanes(*parts):
        cols = [jnp.zeros((seq, x), jnp.float32) if isinstance(x, int) else x for x in parts]
        out = jnp.concatenate(cols, axis=1)
        assert out.shape == (seq, LANES)
        return out

    hm, hs = MLA_ROPE // 2, SWA_HEAD_DIM // 2
    tail = MLA_PAD - MLA_NOPE - MLA_ROPE
    tables = [
        scale * lanes(jnp.ones((seq, MLA_NOPE), jnp.float32), mc, mc, tail),
        scale * lanes(MLA_NOPE, -ms, hm, tail),
        scale * lanes(MLA_NOPE, hm, ms, tail),
        lanes(mc, mc, LANES - MLA_ROPE),
        lanes(-ms, hm, LANES - MLA_ROPE),
        lanes(hm, ms, LANES - MLA_ROPE),
        lanes(sc, sc, sc, sc),
        lanes(-ss, hs, -ss, hs),
        lanes(hs, ss, hs, ss),
    ]
    return jnp.concatenate(tables, axis=1)


def _encode(x, lw, depth, final_norm):
    batch, seq, _ = x.shape
    tables = _rope_tables(seq)
    h = x.reshape(batch * seq, D_MODEL)
    for layer in range(depth):
        h = _ffn(h, lw["ffn1_norm"], lw["ffn1_w_gu"], lw["ffn1_w_down"], final_norm, layer, False)
        q, k, v, yb, sq, sk, sv = _proj(h, tables, lw, layer, seq)
        oa = _mla(q, k, v, batch, seq)
        oc = _swa(lw["sink"][layer], sq, sk, sv, seq)
        h = _merge(h, oa, yb, oc, lw, layer)
        h = _ffn(h, lw["ffn2_norm"], lw["ffn2_w_gu"], lw["ffn2_w_down"], final_norm, layer,
                 layer == depth - 1)
    return h.reshape(batch, seq, D_MODEL)


def kernel(x_prompt, x_sample, ffn1_norm, ffn1_w_gu, ffn1_w_down, mix_norm, w_in, mla_q_norm,
           mla_w_uq, mla_kv_norm, mla_w_ukv, mla_w_o, conv_w, conv_w_o, swa_sink, swa_w_o, w_o,
           ffn2_norm, ffn2_w_gu, ffn2_w_down, final_norm):
    depth = w_in.shape[0]
    w_main, w_gate, w_q, w_kv = _prepare_weights(_bf16(w_in), _bf16(mla_w_uq), _bf16(mla_w_ukv))
    lw = {
        "ffn1_norm": ffn1_norm[:, None, :], "ffn1_w_gu": _bf16(ffn1_w_gu), "ffn1_w_down": _bf16(ffn1_w_down),
        "ffn2_norm": ffn2_norm[:, None, :], "ffn2_w_gu": _bf16(ffn2_w_gu), "ffn2_w_down": _bf16(ffn2_w_down),
        "mix_norm": mix_norm[:, None, :], "w_main": w_main, "w_gate": w_gate,
        "q_norm": mla_q_norm[:, None, :], "w_q": w_q, "kv_norm": mla_kv_norm[:, None, :], "w_kv": w_kv,
        "conv_w": jnp.pad(conv_w, ((0, 0), (0, HALO - conv_w.shape[1]), (0, 0))), "sink": swa_sink,
        "w_a": _bf16(mla_w_o), "w_b": _bf16(conv_w_o), "w_c": _bf16(swa_w_o), "w_o": _bf16(w_o),
    }
    fin = final_norm[None, :]
    return (_encode(x_prompt, lw, depth, fin), _encode(x_sample, lw, depth, fin))
```

```python
import functools

import jax
import jax.numpy as jnp
from jax import lax
from jax.experimental import pallas as pl
from jax.experimental.pallas import tpu as pltpu

D_MODEL = 1024
D_FF = 2816
NORM_EPS = 1e-6
ROPE_THETA = 10000.0

MLA_HEADS = 8
MLA_Q_RANK = 256
MLA_KV_RANK = 128
MLA_NOPE = 64
MLA_ROPE = 32
MLA_V = 64

CONV_WIDTH = 512

SWA_HEADS = 8
SWA_KV_HEADS = 2
SWA_HEAD_DIM = 64
SWA_WINDOW = 128
SWA_GROUP = SWA_HEADS // SWA_KV_HEADS

LANES = 128
HALF = LANES // 2
MLA_PAD = LANES
MASK_VALUE = -1e30
LOG2_E = 1.4426950408889634

TOKEN_TILE = 512
FF_CHUNK = 1408
MLA_TQ = 256
MLA_S_BYTES = 8 * 1024 * 1024
MLA_TK = 2048
MLA_QBLOCK = 4096
MLA_GROUP = 4
SWA_TILE = 512
SUBLANES = 8
HALO = SUBLANES
VMEM_LIMIT = 56 * 1024 * 1024

C_QLAT = 0
C_KVLAT = C_QLAT + MLA_Q_RANK
C_CONV = C_KVLAT + MLA_KV_RANK
C_SQ = C_CONV + 3 * CONV_WIDTH
C_SK = C_SQ + SWA_HEADS * SWA_HEAD_DIM
C_SV = C_SK + SWA_KV_HEADS * SWA_HEAD_DIM
C_KR = C_SV + SWA_KV_HEADS * SWA_HEAD_DIM
C_MAIN = C_KR + LANES

T_CQ, T_SQA, T_SQB, T_CK, T_SKA, T_SKB, T_CS, T_SSA, T_SSB = range(9)
N_TABLES = 9


def _bf16(x):
    return x.astype(jnp.bfloat16)


def _dot(a, b):
    return jnp.dot(a, b, preferred_element_type=jnp.float32)


def _dot_nt(a, b):
    return lax.dot_general(a, b, (((1,), (1,)), ((), ())), preferred_element_type=jnp.float32)


def _rmsnorm(x, g):
    return x * lax.rsqrt(jnp.mean(x * x, axis=-1, keepdims=True) + NORM_EPS) * g


def _sigmoid(x):
    return 1.0 / (1.0 + jnp.exp(-x))


def _const_spec(shape):
    return pl.BlockSpec(shape, lambda *_: (0,) * len(shape), pipeline_mode=pl.Buffered(1))


def _layer_spec(shape, layer):
    return pl.BlockSpec((None,) + shape, lambda *_: (layer,) + (0,) * len(shape),
                        pipeline_mode=pl.Buffered(1))


def _params(sem):
    return pltpu.CompilerParams(dimension_semantics=sem, vmem_limit_bytes=VMEM_LIMIT)


def _ffn_body(x_ref, g_ref, wgu_ref, wd_ref, fin_ref, o_ref, h_ref, *, final):
    x = x_ref[...]
    xn = _bf16(_rmsnorm(x, g_ref[...]))
    for c in range(D_FF // FF_CHUNK):
        a = _dot(xn, wgu_ref[:, c * FF_CHUNK:(c + 1) * FF_CHUNK])
        b = _dot(xn, wgu_ref[:, D_FF + c * FF_CHUNK:D_FF + (c + 1) * FF_CHUNK])
        h_ref[:, c * FF_CHUNK:(c + 1) * FF_CHUNK] = _bf16(a * _sigmoid(a) * b)
    y = x + 0.5 * _dot(h_ref[...], wd_ref[...])
    if final:
        y = _rmsnorm(y, fin_ref[...])
    o_ref[...] = y


def _ffn(x, g, wgu, wd, fin, layer, final):
    t = x.shape[0]
    tok = pl.BlockSpec((TOKEN_TILE, D_MODEL), lambda i: (i, 0))
    return pl.pallas_call(
        functools.partial(_ffn_body, final=final),
        grid=(t // TOKEN_TILE,),
        in_specs=[tok, _layer_spec((1, D_MODEL), layer), _layer_spec((D_MODEL, 2 * D_FF), layer),
                  _layer_spec((D_FF, D_MODEL), layer), _const_spec((1, D_MODEL))],
        out_specs=tok,
        out_shape=jax.ShapeDtypeStruct((t, D_MODEL), jnp.float32),
        scratch_shapes=[pltpu.VMEM((TOKEN_TILE, D_FF), jnp.bfloat16)],
        compiler_params=_params(("parallel",)),
        name="ffn_final" if final else "ffn",
    )(x, g, wgu, wd, fin)


def _proj_body(x_ref, xp_ref, xn_ref, tab_ref, g_ref, wm_ref, qn_ref, wq_ref, kvn_ref, wkv_ref,
               cw_ref, q_ref, k_ref, v_ref, yb_ref, sq_ref, sk_ref, sv_ref, *, seq):
    i = pl.program_id(0)
    g = g_ref[...]
    u = _bf16(_rmsnorm(x_ref[...], g))

    def tab(j):
        return tab_ref[:, j * LANES:(j + 1) * LANES]

    def proj(c0, c1):
        return _dot(u, wm_ref[:, c0:c1])

    def rope(x, cos, sin_a, sin_b, half):
        return (x * tab(cos) + pltpu.roll(x, LANES - half, axis=1) * tab(sin_a)
                + pltpu.roll(x, half, axis=1) * tab(sin_b))

    rq = _bf16(_rmsnorm(proj(C_QLAT, C_KVLAT), qn_ref[...]))
    qq = _dot(rq, wq_ref[...])
    for h in range(MLA_HEADS):
        q_ref[h] = _bf16(rope(qq[:, h * MLA_PAD:(h + 1) * MLA_PAD], T_CQ, T_SQA, T_SQB, MLA_ROPE // 2))

    rkv = _bf16(_rmsnorm(proj(C_KVLAT, C_CONV), kvn_ref[...]))
    kvv = _dot(rkv, wkv_ref[...])
    kr = rope(proj(C_KR, C_MAIN), T_CK, T_SKA, T_SKB, MLA_ROPE // 2)
    kr = pltpu.roll(kr, MLA_NOPE, axis=1)
    ones = jnp.where(lax.broadcasted_iota(jnp.int32, (1, LANES), 1) >= HALF, 1.0, 0.0)
    for h in range(MLA_HEADS):
        sl = slice(h * MLA_PAD, (h + 1) * MLA_PAD)
        k_ref[h] = _bf16(kvv[:, sl] + kr)
        vh = kvv[:, MLA_HEADS * MLA_PAD + h * LANES:MLA_HEADS * MLA_PAD + (h + 1) * LANES] + ones
        v_ref[h] = _bf16(vh.T)

    b_gate = proj(C_CONV, C_CONV + CONV_WIDTH)
    xh = jnp.concatenate([xp_ref[...], xn_ref[...]], axis=0)
    uh = _bf16(_rmsnorm(xh, g))
    cx = _dot(jnp.concatenate([u, uh], axis=0), wm_ref[:, C_CONV + CONV_WIDTH:C_SQ])
    zz = cx[:, :CONV_WIDTH] * cx[:, CONV_WIDTH:]
    z = zz[:TOKEN_TILE]
    zh = zz[TOKEN_TILE:]
    row0 = i * TOKEN_TILE
    has_prev = (row0 % seq) != 0
    has_next = ((row0 + TOKEN_TILE) % seq) != 0
    z_prev = jnp.where(has_prev, zh[HALO - 1:HALO, :], 0.0)
    z_next = jnp.where(has_next, zh[HALO:HALO + 1, :], 0.0)
    rows = lax.broadcasted_iota(jnp.int32, (TOKEN_TILE, CONV_WIDTH), 0)
    z_up = jnp.where(rows == 0, z_prev, pltpu.roll(z, 1, axis=0))
    z_dn = jnp.where(rows == TOKEN_TILE - 1, z_next, pltpu.roll(z, TOKEN_TILE - 1, axis=0))
    y = z_up * cw_ref[0:1, :] + z * cw_ref[1:2, :] + z_dn * cw_ref[2:3, :]
    yb_ref[...] = _bf16(b_gate * y)

    sq = proj(C_SQ, C_SK)
    scale = SWA_HEAD_DIM ** -0.5 * LOG2_E
    for p in range(SWA_HEADS * SWA_HEAD_DIM // LANES):
        sl = slice(p * LANES, (p + 1) * LANES)
        sq_ref[:, sl] = _bf16(rope(sq[:, sl], T_CS, T_SSA, T_SSB, SWA_HEAD_DIM // 2) * scale)
    sk_ref[...] = _bf16(rope(proj(C_SK, C_SV), T_CS, T_SSA, T_SSB, SWA_HEAD_DIM // 2))
    sv_ref[...] = _bf16(proj(C_SV, C_KR))


def _proj(x, tables, lw, layer, seq):
    t = x.shape[0]
    nt = t // TOKEN_TILE
    tiles_per_seq = seq // TOKEN_TILE
    halo_per_tile = TOKEN_TILE // HALO
    n_halo = t // HALO

    def tok(width):
        return pl.BlockSpec((TOKEN_TILE, width), lambda i: (i, 0))

    def out(width):
        return jax.ShapeDtypeStruct((t, width), jnp.bfloat16)

    in_specs = [
        tok(D_MODEL),
        pl.BlockSpec((HALO, D_MODEL), lambda i: (jnp.maximum(i * halo_per_tile - 1, 0), 0)),
        pl.BlockSpec((HALO, D_MODEL), lambda i: (jnp.minimum((i + 1) * halo_per_tile, n_halo - 1), 0)),
        pl.BlockSpec((TOKEN_TILE, N_TABLES * LANES), lambda i: (i % tiles_per_seq, 0)),
        _layer_spec((1, D_MODEL), layer),
        _layer_spec((D_MODEL, C_MAIN), layer),
        _layer_spec((1, MLA_Q_RANK), layer),
        _layer_spec((MLA_Q_RANK, MLA_HEADS * MLA_PAD), layer),
        _layer_spec((1, MLA_KV_RANK), layer),
        _layer_spec((MLA_KV_RANK, MLA_HEADS * (MLA_PAD + LANES)), layer),
        _layer_spec((HALO, CONV_WIDTH), layer),
    ]
    widths = [CONV_WIDTH, SWA_HEADS * SWA_HEAD_DIM, SWA_KV_HEADS * SWA_HEAD_DIM,
              SWA_KV_HEADS * SWA_HEAD_DIM]
    head_spec = pl.BlockSpec((MLA_HEADS, TOKEN_TILE, LANES), lambda i: (0, i, 0))
    head_shape = jax.ShapeDtypeStruct((MLA_HEADS, t, LANES), jnp.bfloat16)
    return pl.pallas_call(
        functools.partial(_proj_body, seq=seq),
        grid=(nt,),
        in_specs=in_specs,
        out_specs=[head_spec, head_spec, pl.BlockSpec((MLA_HEADS, LANES, TOKEN_TILE), lambda i: (0, 0, i))]
        + [tok(w) for w in widths],
        out_shape=[head_shape, head_shape, jax.ShapeDtypeStruct((MLA_HEADS, LANES, t), jnp.bfloat16)]
        + [out(w) for w in widths],
        compiler_params=_params(("parallel",)),
        name="proj",
    )(x, x, x, tables, lw["mix_norm"], lw["w_main"], lw["q_norm"], lw["w_q"], lw["kv_norm"],
      lw["w_kv"], lw["conv_w"])


def _mla_body(q_ref, k_ref, vt_ref, o_ref, s0_ref, s1_ref, m0_ref, m1_ref, o_scr, *, seq, sub):
    n_chunks = seq // MLA_TK
    item_rows = sub * MLA_TQ
    n_items = (MLA_QBLOCK // item_rows) * MLA_GROUP
    assert n_items % 2 == 0 and n_items >= 4
    low = lax.broadcasted_iota(jnp.int32, (MLA_TQ, LANES), 1) < HALF
    bufs = ((s0_ref, m0_ref), (s1_ref, m1_ref))

    def rows_of(t, u):
        return pl.ds(pl.multiple_of((t // MLA_GROUP) * item_rows + u * MLA_TQ, MLA_TQ), MLA_TQ)

    def pass_a(t, slot):
        s_ref, m_ref = bufs[slot]
        head = t % MLA_GROUP
        for u in range(sub):
            q = q_ref[head, rows_of(t, u), :]
            mm = None
            for c in range(n_chunks):
                s = _dot_nt(k_ref[head, c * MLA_TK:(c + 1) * MLA_TK, :], q)
                s_ref[u, c] = s
                cm = jnp.max(s, axis=0, keepdims=True)
                mm = cm if mm is None else jnp.maximum(mm, cm)
            m_ref[u] = jnp.broadcast_to(mm, (SUBLANES, MLA_TQ))

    def pass_b(t, slot):
        s_ref, m_ref = bufs[slot]
        head = t % MLA_GROUP
        odd = (head % 2) == 1
        valid = jnp.logical_xor(low, odd)
        for u in range(sub):
            m = m_ref[u][:1, :]
            res = None
            for c in range(n_chunks):
                p = _bf16(jnp.exp2(s_ref[u, c] - m))
                pv = _dot(vt_ref[head, :, c * MLA_TK:(c + 1) * MLA_TK], p)
                res = pv if res is None else res + pv
            r = res.T
            val = r / pltpu.roll(r, HALF, axis=1)
            val = jnp.where(odd, pltpu.roll(val, HALF, axis=1), val)
            new = jnp.where(valid, val, jnp.where(odd, o_scr[u], 0.0))
            o_scr[u] = new
            o_ref[head // 2, rows_of(t, u), :] = _bf16(new)

    o_scr[...] = jnp.zeros((sub, MLA_TQ, LANES), jnp.float32)
    pass_a(0, 0)

    def pair(j, carry):
        t = 2 * j
        pass_a(t + 1, 1)
        pass_b(t, 0)
        pass_a(t + 2, 0)
        pass_b(t + 1, 1)
        return carry

    lax.fori_loop(0, n_items // 2 - 1, pair, 0)
    pass_a(n_items - 1, 1)
    pass_b(n_items - 2, 0)
    pass_b(n_items - 1, 1)


def _mla(q, k, vt, batch, seq):
    t = q.shape[1]
    sub = max(1, min(MLA_QBLOCK // MLA_TQ, MLA_S_BYTES // (2 * 4 * seq * MLA_TQ)))
    nqb = seq // MLA_QBLOCK
    groups = MLA_HEADS // MLA_GROUP
    score = pltpu.VMEM((sub, seq // MLA_TK, MLA_TK, MLA_TQ), jnp.float32)
    row_max = pltpu.VMEM((sub, SUBLANES, MLA_TQ), jnp.float32)
    return pl.pallas_call(
        functools.partial(_mla_body, seq=seq, sub=sub),
        grid=(batch, groups, nqb),
        in_specs=[pl.BlockSpec((MLA_GROUP, MLA_QBLOCK, LANES), lambda b, g, i: (g, b * nqb + i, 0)),
                  pl.BlockSpec((MLA_GROUP, seq, LANES), lambda b, g, i: (g, b, 0),
                               pipeline_mode=pl.Buffered(1)),
                  pl.BlockSpec((MLA_GROUP, LANES, seq), lambda b, g, i: (g, 0, b),
                               pipeline_mode=pl.Buffered(1))],
        out_specs=pl.BlockSpec((MLA_GROUP // 2, MLA_QBLOCK, LANES), lambda b, g, i: (g, b * nqb + i, 0)),
        out_shape=jax.ShapeDtypeStruct((MLA_HEADS // 2, t, LANES), jnp.bfloat16),
        scratch_shapes=[score, score, row_max, row_max, pltpu.VMEM((sub, MLA_TQ, LANES), jnp.float32)],
        compiler_params=_params(("parallel", "parallel", "arbitrary")),
        name="mla_attn",
    )(q, k, vt)


def _swa_body(sink_ref, bias_ref, q_ref, kp_ref, kc_ref, kn_ref, vp_ref, vc_ref, vn_ref, o_ref,
              kbuf, vbuf, *, seq):
    w = SWA_WINDOW
    n_sub = SWA_TILE // w
    i = pl.program_id(0)
    kbuf[0:w, :] = kp_ref[...]
    kbuf[w:w + SWA_TILE, :] = kc_ref[...]
    kbuf[w + SWA_TILE:, :] = kn_ref[...]
    vbuf[0:w, :] = vp_ref[...]
    vbuf[w:w + SWA_TILE, :] = vc_ref[...]
    vbuf[w + SWA_TILE:, :] = vn_ref[...]

    rows = SWA_GROUP * w
    head_row = lax.broadcasted_iota(jnp.int32, (rows, 1), 0) // w
    lane_q = lax.broadcasted_iota(jnp.int32, (w, LANES), 1)
    pos0 = (i * SWA_TILE) % seq
    first = jnp.where(pos0 == 0, 1, 0)
    last = jnp.where(pos0 + SWA_TILE == seq, 2, 0)

    for n in range(n_sub):
        bias = bias_ref[first if n == 0 else (last if n == n_sub - 1 else 0)]
        kb = kbuf[n * w:(n + 3) * w, :]
        vb = vbuf[n * w:(n + 3) * w, :]
        outs = [None] * SWA_HEADS
        for g in range(SWA_KV_HEADS):
            keep = (lane_q >= HALF) if g else (lane_q < HALF)
            parts = []
            sink = jnp.zeros((rows, 1), jnp.float32)
            for a in range(SWA_GROUP):
                head = g * SWA_GROUP + a
                blk = q_ref[n * w:(n + 1) * w, (head // 2) * LANES:(head // 2 + 1) * LANES]
                blk = blk.astype(jnp.float32)
                if head % 2 != g:
                    blk = pltpu.roll(blk, HALF, axis=1)
                parts.append(_bf16(jnp.where(keep, blk, 0.0)))
                sink = jnp.where(head_row == a, sink_ref[head] * LOG2_E, sink)
            qs = jnp.concatenate(parts, axis=0)
            s = jnp.minimum(_dot_nt(qs, kb), bias)
            m = jnp.maximum(jnp.max(s, axis=-1, keepdims=True), sink)
            p = jnp.exp2(s - m)
            denom = jnp.sum(p, axis=-1, keepdims=True) + jnp.exp2(sink - m)
            pv = _dot(_bf16(p), vb) * (1.0 / denom)
            for a in range(SWA_GROUP):
                head = g * SWA_GROUP + a
                blk = pv[a * w:(a + 1) * w, :]
                if head % 2 != g:
                    blk = pltpu.roll(blk, HALF, axis=1)
                outs[head] = blk
        for pr in range(SWA_HEADS // 2):
            o_ref[n * w:(n + 1) * w, pr * LANES:(pr + 1) * LANES] = _bf16(
                jnp.where(lane_q < HALF, outs[2 * pr], outs[2 * pr + 1]))


def _swa_bias():
    w = SWA_WINDOW
    rows = SWA_GROUP * w
    r = jnp.arange(rows)[:, None] % w
    c = jnp.arange(3 * w)[None, :]
    band = jnp.abs(c - w - r) <= SWA_WINDOW
    visible = jnp.stack([band, band & (c >= w), band & (c < 2 * w)])
    return jnp.where(visible, jnp.inf, MASK_VALUE).astype(jnp.float32)


def _swa(sink, q, k, v, seq):
    t = q.shape[0]
    w = SWA_WINDOW
    per_tile = SWA_TILE // w
    assert per_tile >= 2
    n_blocks = t // w
    kvw = SWA_KV_HEADS * SWA_HEAD_DIM

    prev = pl.BlockSpec((w, kvw), lambda i: (jnp.maximum(i * per_tile - 1, 0), 0))
    cur = pl.BlockSpec((SWA_TILE, kvw), lambda i: (i, 0))
    nxt = pl.BlockSpec((w, kvw), lambda i: (jnp.minimum((i + 1) * per_tile, n_blocks - 1), 0))
    qspec = pl.BlockSpec((SWA_TILE, SWA_HEADS * SWA_HEAD_DIM), lambda i: (i, 0))
    return pl.pallas_call(
        functools.partial(_swa_body, seq=seq),
        grid=(t // SWA_TILE,),
        in_specs=[pl.BlockSpec(memory_space=pltpu.SMEM), _const_spec((3, SWA_GROUP * w, 3 * w)),
                  qspec, prev, cur, nxt, prev, cur, nxt],
        out_specs=qspec,
        out_shape=jax.ShapeDtypeStruct((t, SWA_HEADS * SWA_HEAD_DIM), jnp.bfloat16),
        scratch_shapes=[pltpu.VMEM((SWA_TILE + 2 * w, kvw), jnp.bfloat16),
                        pltpu.VMEM((SWA_TILE + 2 * w, kvw), jnp.bfloat16)],
        compiler_params=_params(("parallel",)),
        name="swa_attn",
    )(sink, _swa_bias(), q, k, k, k, v, v, v)


def _merge_body(x_ref, oa_ref, yb_ref, oc_ref, g_ref, wg_ref, wa_ref, wb_ref, wc_ref, wo_ref, o_ref):
    x = x_ref[...]
    u = _bf16(_rmsnorm(x, g_ref[...]))
    oa = jnp.concatenate([oa_ref[p] for p in range(MLA_HEADS // 2)], axis=1)
    branches = ((oa, wa_ref), (yb_ref[...], wb_ref), (oc_ref[...], wc_ref))
    merged = None
    for b, (y, w_ref) in enumerate(branches):
        gate = _sigmoid(_dot(u, wg_ref[:, b * D_MODEL:(b + 1) * D_MODEL]))
        term = gate * _dot(y, w_ref[...])
        merged = term if merged is None else merged + term
    o_ref[...] = x + _dot(_bf16(merged), wo_ref[...])


def _merge(x, oa, yb, oc, lw, layer):
    t = x.shape[0]

    def tok(width):
        return pl.BlockSpec((TOKEN_TILE, width), lambda i: (i, 0))

    half = MLA_HEADS * MLA_V
    return pl.pallas_call(
        _merge_body,
        grid=(t // TOKEN_TILE,),
        in_specs=[tok(D_MODEL), pl.BlockSpec((MLA_HEADS // 2, TOKEN_TILE, LANES), lambda i: (0, i, 0)),
                  tok(CONV_WIDTH), tok(SWA_HEADS * SWA_HEAD_DIM),
                  _layer_spec((1, D_MODEL), layer), _layer_spec((D_MODEL, 3 * D_MODEL), layer),
                  _layer_spec((half, D_MODEL), layer), _layer_spec((CONV_WIDTH, D_MODEL), layer),
                  _layer_spec((SWA_HEADS * SWA_HEAD_DIM, D_MODEL), layer),
                  _layer_spec((D_MODEL, D_MODEL), layer)],
        out_specs=tok(D_MODEL),
        out_shape=jax.ShapeDtypeStruct((t, D_MODEL), jnp.float32),
        compiler_params=_params(("parallel",)),
        name="merge",
    )(x, oa, yb, oc, lw["mix_norm"], lw["w_gate"], lw["w_a"], lw["w_b"], lw["w_c"], lw["w_o"])


def _pad_cols(w, width):
    return jnp.pad(w, [(0, 0)] * (w.ndim - 1) + [(0, width - w.shape[-1])])


def _prepare_weights(w_in, mla_w_uq, mla_w_ukv):
    depth = w_in.shape[0]
    o = 0
    q_lat = w_in[..., o:o + MLA_Q_RANK]; o += MLA_Q_RANK
    kv_lat = w_in[..., o:o + MLA_KV_RANK]; o += MLA_KV_RANK
    k_rope = w_in[..., o:o + MLA_ROPE]; o += MLA_ROPE
    conv = w_in[..., o:o + 3 * CONV_WIDTH]; o += 3 * CONV_WIDTH
    sq = w_in[..., o:o + SWA_HEADS * SWA_HEAD_DIM]; o += SWA_HEADS * SWA_HEAD_DIM
    sk = w_in[..., o:o + SWA_KV_HEADS * SWA_HEAD_DIM]; o += SWA_KV_HEADS * SWA_HEAD_DIM
    sv = w_in[..., o:o + SWA_KV_HEADS * SWA_HEAD_DIM]; o += SWA_KV_HEADS * SWA_HEAD_DIM
    w_gate = w_in[..., o:]
    w_main = jnp.concatenate([q_lat, kv_lat, conv, sq, sk, sv, _pad_cols(k_rope, LANES)], axis=-1)

    uq = mla_w_uq.reshape(depth, MLA_Q_RANK, MLA_HEADS, MLA_NOPE + MLA_ROPE)
    nope, rope = uq[..., :MLA_NOPE], uq[..., MLA_NOPE:]
    zeros = jnp.zeros_like
    pad = zeros(rope)
    w_q = jnp.concatenate([nope, rope, pad], axis=-1).reshape(depth, MLA_Q_RANK, -1)

    ukv = mla_w_ukv.reshape(depth, MLA_KV_RANK, MLA_HEADS, MLA_NOPE + MLA_V)
    k_nope, v = ukv[..., :MLA_NOPE], ukv[..., MLA_NOPE:]
    v_pad = jnp.concatenate([v, zeros(v)], axis=-1)
    w_kv = jnp.concatenate([
        jnp.concatenate([k_nope, zeros(k_nope)], axis=-1).reshape(depth, MLA_KV_RANK, -1),
        v_pad.reshape(depth, MLA_KV_RANK, -1)], axis=-1)
    return w_main, w_gate, w_q, w_kv


def _rope_tables(seq):
    def cos_sin(dim):
        inv = 1.0 / (ROPE_THETA ** (jnp.arange(0, dim, 2, dtype=jnp.float32) / dim))
        ang = jnp.arange(seq, dtype=jnp.float32)[:, None] * inv[None, :]
        return jnp.cos(ang), jnp.sin(ang)

    mc, ms = cos_sin(MLA_ROPE)
    sc, ss = cos_sin(SWA_HEAD_DIM)
    scale = (MLA_NOPE + MLA_ROPE) ** -0.5 * LOG2_E

    def lanes(*parts):
        cols = [jnp.zeros((seq, x), jnp.float32) if isinstance(x, int) else x for x in parts]
        out = jnp.concatenate(cols, axis=1)
        assert out.shape == (seq, LANES)
        return out

    hm, hs = MLA_ROPE // 2, SWA_HEAD_DIM // 2
    tail = MLA_PAD - MLA_NOPE - MLA_ROPE
    tables = [
        scale * lanes(jnp.ones((seq, MLA_NOPE), jnp.float32), mc, mc, tail),
        scale * lanes(MLA_NOPE, -ms, hm, tail),
        scale * lanes(MLA_NOPE, hm, ms, tail),
        lanes(mc, mc, LANES - MLA_ROPE),
        lanes(-ms, hm, LANES - MLA_ROPE),
        lanes(hm, ms, LANES - MLA_ROPE),
        lanes(sc, sc, sc, sc),
        lanes(-ss, hs, -ss, hs),
        lanes(hs, ss, hs, ss),
    ]
    return jnp.concatenate(tables, axis=1)


def _encode(x, lw, depth, final_norm):
    batch, seq, _ = x.shape
    tables = _rope_tables(seq)
    h = x.reshape(batch * seq, D_MODEL)
    for layer in range(depth):
        h = _ffn(h, lw["ffn1_norm"], lw["ffn1_w_gu"], lw["ffn1_w_down"], final_norm, layer, False)
        q, k, v, yb, sq, sk, sv = _proj(h, tables, lw, layer, seq)
        oa = _mla(q, k, v, batch, seq)
        oc = _swa(lw["sink"][layer], sq, sk, sv, seq)
        h = _merge(h, oa, yb, oc, lw, layer)
        h = _ffn(h, lw["ffn2_norm"], lw["ffn2_w_gu"], lw["ffn2_w_down"], final_norm, layer,
                 layer == depth - 1)
    return h.reshape(batch, seq, D_MODEL)


def kernel(x_prompt, x_sample, ffn1_norm, ffn1_w_gu, ffn1_w_down, mix_norm, w_in, mla_q_norm,
           mla_w_uq, mla_kv_norm, mla_w_ukv, mla_w_o, conv_w, conv_w_o, swa_sink, swa_w_o, w_o,
           ffn2_norm, ffn2_w_gu, ffn2_w_down, final_norm):
    depth = w_in.shape[0]
    w_main, w_gate, w_q, w_kv = _prepare_weights(_bf16(w_in), _bf16(mla_w_uq), _bf16(mla_w_ukv))
    lw = {
        "ffn1_norm": ffn1_norm[:, None, :], "ffn1_w_gu": _bf16(ffn1_w_gu), "ffn1_w_down": _bf16(ffn1_w_down),
        "ffn2_norm": ffn2_norm[:, None, :], "ffn2_w_gu": _bf16(ffn2_w_gu), "ffn2_w_down": _bf16(ffn2_w_down),
        "mix_norm": mix_norm[:, None, :], "w_main": w_main, "w_gate": w_gate,
        "q_norm": mla_q_norm[:, None, :], "w_q": w_q, "kv_norm": mla_kv_norm[:, None, :], "w_kv": w_kv,
        "conv_w": jnp.pad(conv_w, ((0, 0), (0, HALO - conv_w.shape[1]), (0, 0))), "sink": swa_sink,
        "w_a": _bf16(mla_w_o), "w_b": _bf16(conv_w_o), "w_c": _bf16(swa_w_o), "w_o": _bf16(w_o),
    }
    fin = final_norm[None, :]
    return (_encode(x_prompt, lw, depth, fin), _encode(x_sample, lw, depth, fin))
```
